```python
import math
import jax
import jax.numpy as jnp
from jax import lax
import numpy as np

D_MODEL = 1024
BATCH = 8
SEQ = 8192
DEPTH = 4

BRANCH_WIDTH = D_MODEL // 2
HEAD_DIM = 128
N_HEADS = BRANCH_WIDTH // HEAD_DIM
N_BRANCHES = 3
CONV_WIDTH = 4
CHUNK = 64
SB_BLOCK = 128
N_META = 16
FRONT = SB_BLOCK
PAD_FRONT = FRONT - N_META
NORM_EPS = 1e-6
SPLIT_SIZES = (BRANCH_WIDTH, BRANCH_WIDTH, BRANCH_WIDTH, BRANCH_WIDTH,
               3 * BRANCH_WIDTH, BRANCH_WIDTH, N_HEADS, N_HEADS,
               BRANCH_WIDTH, BRANCH_WIDTH, BRANCH_WIDTH, BRANCH_WIDTH,
               N_BRANCHES * D_MODEL)
N_IN = 12 * BRANCH_WIDTH + 2 * N_HEADS + N_BRANCHES * D_MODEL

kernel_name = 'meta_sb_gdn_hgrn2_gated_hybrid'


def rms_norm(x, w):
    xf = x.astype(jnp.float32)
    y = xf * lax.rsqrt(jnp.mean(xf * xf, axis=-1, keepdims=True) + NORM_EPS)
    return (y * w.astype(jnp.float32)).astype(x.dtype)


def l2_norm(x):
    return x * lax.rsqrt(jnp.sum(x * x, axis=-1, keepdims=True) + NORM_EPS)


def causal_conv(x, w):
    t_len = x.shape[1]
    xp = jnp.pad(x, ((0, 0), (CONV_WIDTH - 1, 0), (0, 0)))
    y = xp[:, 0:t_len] * w[0]
    for i in range(1, CONV_WIDTH):
        y = y + xp[:, i:i + t_len] * w[i]
    return y


def stick_breaking_attention(q, k, v, key_valid):
    t_len, d = q.shape[2], q.shape[3]
    scale = d ** -0.5
    outs = []
    for blk in range(t_len // SB_BLOCK):
        q0 = blk * SB_BLOCK
        q1 = q0 + SB_BLOCK
        z = jnp.einsum('bhqd,bhkd->bhqk', q[:, :, q0:q1], k[:, :, :q1]) * scale
        t_idx = jnp.arange(q0, q1)[:, None]
        s_idx = jnp.arange(q1)[None, :]
        mask = (s_idx < t_idx) & key_valid[None, :q1]
        log_keep = jnp.where(mask, jax.nn.log_sigmoid(-z), 0.0)
        log_passed = lax.cumsum(log_keep, axis=3, reverse=True) - log_keep
        attn = jnp.where(mask, jnp.exp(jax.nn.log_sigmoid(z) + log_passed), 0.0)
        outs.append(jnp.einsum('bhqk,bhkd->bhqd', attn, v[:, :, :q1]))
    return jnp.concatenate(outs, axis=2)


def to_chunks(a):
    bsz, t_len, h = a.shape[0], a.shape[1], a.shape[2]
    a = a.reshape((bsz, t_len // CHUNK, CHUNK, h) + a.shape[3:])
    return jnp.moveaxis(a, 3, 1)


def from_chunks(o):
    bsz, h, n, c, dv = o.shape
    return jnp.moveaxis(o, 1, 3).reshape(bsz, n * c, h, dv)


def gated_delta_rule_chunked(q, k, v, beta, g):
    bsz, _, h, dk = q.shape
    dv = v.shape[-1]
    q = to_chunks(q) * (dk ** -0.5)
    k = to_chunks(k)
    v = to_chunks(v)
    beta = to_chunks(beta)
    G = jnp.cumsum(to_chunks(g), axis=-1)
    causal = np.tril(np.ones((CHUNK, CHUNK), dtype=bool))
    strict = np.tril(np.ones((CHUNK, CHUNK), dtype=bool), k=-1)
    decay = jnp.exp(jnp.where(causal, G[..., :, None] - G[..., None, :], -jnp.inf))
    k_beta = k * beta[..., None]
    m = jnp.where(strict, jnp.einsum('bhncd,bhnsd->bhncs', k_beta, k) * decay, 0.0)
    t_mat = m + jnp.eye(CHUNK, dtype=jnp.float32)
    u = lax.linalg.triangular_solve(t_mat, v * beta[..., None], left_side=True, lower=True, unit_diagonal=True)
    w = lax.linalg.triangular_solve(t_mat, k_beta * jnp.exp(G)[..., None], left_side=True, lower=True, unit_diagonal=True)
    a_qk = jnp.einsum('bhncd,bhnsd->bhncs', q, k) * decay
    q_dec = q * jnp.exp(G)[..., None]
    k_dec = k * jnp.exp(G[..., -1:] - G)[..., None]
    g_last = jnp.exp(G[..., -1])
    xs = tuple(jnp.moveaxis(a, 2, 0) for a in (u, w, a_qk, q_dec, k_dec, g_last))

    def step(state, inp):
        u_c, w_c, aqk_c, qd_c, kd_c, gl_c = inp
        v_new = u_c - jnp.einsum('bhcd,bhde->bhce', w_c, state)
        o = jnp.einsum('bhcd,bhde->bhce', qd_c, state) + jnp.einsum('bhcs,bhse->bhce', aqk_c, v_new)
        state = state * gl_c[..., None, None] + jnp.einsum('bhcd,bhce->bhde', kd_c, v_new)
        return state, o

    s0 = jnp.zeros((bsz, h, dk, dv), jnp.float32)
    _, o = lax.scan(step, s0, xs)
    return from_chunks(jnp.moveaxis(o, 0, 2))


def hgrn2_chunked(q, k, v, g):
    bsz, _, h, dk = q.shape
    dv = v.shape[-1]
    G = jnp.cumsum(to_chunks(g), axis=3)
    xs = tuple(jnp.moveaxis(a, 2, 0) for a in (to_chunks(q), to_chunks(k), to_chunks(v), G))
    causal = np.tril(np.ones((CHUNK, CHUNK), dtype=bool))[:, :, None]

    def step(state, inp):
        q_c, k_c, v_c, g_c = inp
        g_end = g_c[:, :, -1:, :]
        o_inter = jnp.einsum('bhcd,bhde->bhce', q_c * jnp.exp(g_c), state)
        diff = g_c[:, :, :, None, :] - g_c[:, :, None, :, :]
        dec = jnp.exp(jnp.where(causal, diff, -jnp.inf))
        a = jnp.einsum('bhid,bhjd,bhijd->bhij', q_c, k_c, dec)
        o = o_inter + jnp.einsum('bhij,bhje->bhie', a, v_c)
        state = state * jnp.exp(g_end)[:, :, 0, :, None] + jnp.einsum('bhcd,bhce->bhde', k_c * jnp.exp(g_end - g_c), v_c)
        return state, o

    s0 = jnp.zeros((bsz, h, dk, dv), jnp.float32)
    _, o = lax.scan(step, s0, xs)
    return from_chunks(jnp.moveaxis(o, 0, 2))


def hybrid_layer(h, valid, norm_w, w_in, sb_qn, sb_kn, conv_w, a_log, dt_bias, gdn_on, lb, hg_on, w_branch, w_out):
    bsz, t_len, _ = h.shape
    f32 = jnp.float32
    xn = rms_norm(h, norm_w)
    proj = jnp.einsum('btd,dn->btn', xn, w_in)
    split_points = np.cumsum(SPLIT_SIZES)[:-1].tolist()
    (sb_q, sb_k, sb_v, sb_z, gd_qkv, gd_z, gd_b, gd_a,
     hg_q, hg_f, hg_i, hg_z, mix) = jnp.split(proj, split_points, axis=-1)
    vmask = valid[None, :, None].astype(f32)

    def heads(a):
        return a.reshape(bsz, t_len, N_HEADS, HEAD_DIM)

    q = jnp.transpose(rms_norm(heads(sb_q), sb_qn).astype(f32), (0, 2, 1, 3))
    k = jnp.transpose(rms_norm(heads(sb_k), sb_kn).astype(f32), (0, 2, 1, 3))
    v = jnp.transpose(heads(sb_v).astype(f32), (0, 2, 1, 3))
    o_sb = stick_breaking_attention(q, k, v, valid)
    o_sb = jnp.transpose(o_sb, (0, 2, 1, 3)).reshape(bsz, t_len, BRANCH_WIDTH) * jax.nn.silu(sb_z.astype(f32))

    qkv = jax.nn.silu(causal_conv(gd_qkv.astype(f32), conv_w.astype(f32)))
    gq, gk, gv = jnp.split(qkv, 3, axis=-1)
    beta = jax.nn.sigmoid(gd_b.astype(f32)) * vmask
    g = -jnp.exp(a_log.astype(f32)) * jax.nn.softplus(gd_a.astype(f32) + dt_bias.astype(f32))
    o_gd = gated_delta_rule_chunked(l2_norm(heads(gq)), l2_norm(heads(gk)), heads(gv), beta, g)
    o_gd = rms_norm(o_gd, gdn_on).reshape(bsz, t_len, BRANCH_WIDTH) * jax.nn.silu(gd_z.astype(f32))

    lbf = lb.astype(f32)
    f_pre = hg_f.astype(f32)
    forget = lbf + (1.0 - lbf) * jax.nn.sigmoid(f_pre)
    hk = (1.0 - lbf) * jax.nn.sigmoid(-f_pre)
    o_hg = hgrn2_chunked(heads(jax.nn.silu(hg_q.astype(f32))), heads(hk),
                         heads(hg_i.astype(f32) * vmask), heads(jnp.log(forget)))
    o_hg = rms_norm(o_hg, hg_on).reshape(bsz, t_len, BRANCH_WIDTH) * jax.nn.silu(hg_z.astype(f32))

    gates = jax.nn.sigmoid(mix.astype(f32)).reshape(bsz, t_len, N_BRANCHES, D_MODEL)
    y = (gates[:, :, 0] * jnp.einsum('btw,wd->btd', o_sb, w_branch[0])
         + gates[:, :, 1] * jnp.einsum('btw,wd->btd', o_gd, w_branch[1])
         + gates[:, :, 2] * jnp.einsum('btw,wd->btd', o_hg, w_branch[2]))
    out = jnp.einsum('btd,de->bte', y.astype(h.dtype), w_out)
    return h + out.astype(h.dtype)


def _fwd_setup_inputs(seed: int = 0) -> dict:
    key = jax.random.key(seed)
    ks = jax.random.split(key, 16)
    f32 = jnp.float32
    x = jax.random.normal(ks[0], (BATCH, SEQ, D_MODEL), f32)
    meta_tokens = jax.random.normal(ks[1], (N_META, D_MODEL), f32)
    norm_w = 1.0 + 0.02 * jax.random.normal(ks[2], (DEPTH, D_MODEL), f32)
    w_in = jax.random.normal(ks[3], (DEPTH, D_MODEL, N_IN), f32) * D_MODEL ** -0.5
    sb_q_norm = 1.0 + 0.02 * jax.random.normal(ks[4], (DEPTH, HEAD_DIM), f32)
    sb_k_norm = 1.0 + 0.02 * jax.random.normal(ks[5], (DEPTH, HEAD_DIM), f32)
    gdn_conv_w = jax.random.normal(ks[6], (DEPTH, CONV_WIDTH, 3 * BRANCH_WIDTH), f32) * CONV_WIDTH ** -0.5
    gdn_a_log = jnp.log(jax.random.uniform(ks[7], (DEPTH, N_HEADS), f32, 1.0, 16.0))
    dt = jnp.exp(jax.random.uniform(ks[8], (DEPTH, N_HEADS), f32) * (math.log(0.1) - math.log(0.001)) + math.log(0.001))
    gdn_dt_bias = dt + jnp.log(-jnp.expm1(-dt))
    gdn_out_norm = 1.0 + 0.02 * jax.random.normal(ks[9], (DEPTH, HEAD_DIM), f32)
    hgrn_lb_logits = jax.random.normal(ks[10], (DEPTH, BRANCH_WIDTH), f32)
    hgrn_out_norm = 1.0 + 0.02 * jax.random.normal(ks[11], (DEPTH, HEAD_DIM), f32)
    w_branch = jax.random.normal(ks[12], (DEPTH, N_BRANCHES, BRANCH_WIDTH, D_MODEL), f32) * BRANCH_WIDTH ** -0.5
    w_out = jax.random.normal(ks[13], (DEPTH, D_MODEL, D_MODEL), f32) * D_MODEL ** -0.5
    return {'x': x, 'meta_tokens': meta_tokens, 'norm_w': norm_w, 'w_in': w_in,
            'sb_q_norm': sb_q_norm, 'sb_k_norm': sb_k_norm, 'gdn_conv_w': gdn_conv_w,
            'gdn_a_log': gdn_a_log, 'gdn_dt_bias': gdn_dt_bias, 'gdn_out_norm': gdn_out_norm,
            'hgrn_lb_logits': hgrn_lb_logits, 'hgrn_out_norm': hgrn_out_norm,
            'w_branch': w_branch, 'w_out': w_out}


def _fwd_reference(x, meta_tokens, norm_w, w_in, sb_q_norm, sb_k_norm, gdn_conv_w, gdn_a_log,
              gdn_dt_bias, gdn_out_norm, hgrn_lb_logits, hgrn_out_norm, w_branch, w_out):
    bsz = x.shape[0]
    h = jnp.concatenate([
        jnp.zeros((bsz, PAD_FRONT, D_MODEL), x.dtype),
        jnp.broadcast_to(meta_tokens.astype(x.dtype)[None], (bsz, N_META, D_MODEL)),
        x], axis=1)
    t_len = h.shape[1]
    valid = jnp.arange(t_len) >= PAD_FRONT
    p = jax.nn.softmax(hgrn_lb_logits.astype(jnp.float32), axis=0)
    lower_bounds = jnp.cumsum(p, axis=0) - p[0:1]
    for layer in range(DEPTH):
        h = hybrid_layer(h, valid, norm_w[layer], w_in[layer], sb_q_norm[layer], sb_k_norm[layer],
                         gdn_conv_w[layer], gdn_a_log[layer], gdn_dt_bias[layer], gdn_out_norm[layer],
                         lower_bounds[layer], hgrn_out_norm[layer], w_branch[layer], w_out[layer])
    return h[:, FRONT:]


import jax as _jax
import jax.numpy as _jnp

TWIN_FORMAT = 'train_step'
FWD_PARAMS = ['x', 'meta_tokens', 'norm_w', 'w_in', 'sb_q_norm', 'sb_k_norm', 'gdn_conv_w', 'gdn_a_log', 'gdn_dt_bias', 'gdn_out_norm', 'hgrn_lb_logits', 'hgrn_out_norm', 'w_branch', 'w_out']
TWIN_WEIGHTS = ['meta_tokens', 'norm_w', 'w_in', 'sb_q_norm', 'sb_k_norm', 'gdn_conv_w', 'gdn_a_log', 'gdn_dt_bias', 'gdn_out_norm', 'hgrn_lb_logits', 'hgrn_out_norm', 'w_branch', 'w_out']
TWIN_DIFF_INPUT = 'x'
TWIN_INPUTS = ['x', 'meta_tokens', 'norm_w', 'w_in', 'sb_q_norm', 'sb_k_norm', 'gdn_conv_w', 'gdn_a_log', 'gdn_dt_bias', 'gdn_out_norm', 'hgrn_lb_logits', 'hgrn_out_norm', 'w_branch', 'w_out', 'loss_target', 'm_meta_tokens', 'm_norm_w', 'm_w_in', 'm_sb_q_norm', 'm_sb_k_norm', 'm_gdn_conv_w', 'm_gdn_a_log', 'm_gdn_dt_bias', 'm_gdn_out_norm', 'm_hgrn_lb_logits', 'm_hgrn_out_norm', 'm_w_branch', 'm_w_out', 'v_meta_tokens', 'v_norm_w', 'v_w_in', 'v_sb_q_norm', 'v_sb_k_norm', 'v_gdn_conv_w', 'v_gdn_a_log', 'v_gdn_dt_bias', 'v_gdn_out_norm', 'v_hgrn_lb_logits', 'v_hgrn_out_norm', 'v_w_branch', 'v_w_out']
TWIN_OUTPUTS = ['loss', 'grad_x', 'grad_meta_tokens', 'grad_norm_w', 'grad_w_in', 'grad_sb_q_norm', 'grad_sb_k_norm', 'grad_gdn_conv_w', 'grad_gdn_a_log', 'grad_gdn_dt_bias', 'grad_gdn_out_norm', 'grad_hgrn_lb_logits', 'grad_hgrn_out_norm', 'grad_w_branch', 'grad_w_out', 'delta_meta_tokens', 'delta_norm_w', 'delta_w_in', 'delta_sb_q_norm', 'delta_sb_k_norm', 'delta_gdn_conv_w', 'delta_gdn_a_log', 'delta_gdn_dt_bias', 'delta_gdn_out_norm', 'delta_hgrn_lb_logits', 'delta_hgrn_out_norm', 'delta_w_branch', 'delta_w_out', 'new_m_meta_tokens', 'new_m_norm_w', 'new_m_w_in', 'new_m_sb_q_norm', 'new_m_sb_k_norm', 'new_m_gdn_conv_w', 'new_m_gdn_a_log', 'new_m_gdn_dt_bias', 'new_m_gdn_out_norm', 'new_m_hgrn_lb_logits', 'new_m_hgrn_out_norm', 'new_m_w_branch', 'new_m_w_out', 'new_v_meta_tokens', 'new_v_norm_w', 'new_v_w_in', 'new_v_sb_q_norm', 'new_v_sb_k_norm', 'new_v_gdn_conv_w', 'new_v_gdn_a_log', 'new_v_gdn_dt_bias', 'new_v_gdn_out_norm', 'new_v_hgrn_lb_logits', 'new_v_hgrn_out_norm', 'new_v_w_branch', 'new_v_w_out']
TWIN_LEAF_KINDS = {'loss': 'loss', 'grad_x': 'grad_x', 'grad_meta_tokens': 'grad_w', 'grad_norm_w': 'grad_w', 'grad_w_in': 'grad_w', 'grad_sb_q_norm': 'grad_w', 'grad_sb_k_norm': 'grad_w', 'grad_gdn_conv_w': 'grad_w', 'grad_gdn_a_log': 'grad_w', 'grad_gdn_dt_bias': 'grad_w', 'grad_gdn_out_norm': 'grad_w', 'grad_hgrn_lb_logits': 'grad_w', 'grad_hgrn_out_norm': 'grad_w', 'grad_w_branch': 'grad_w', 'grad_w_out': 'grad_w', 'delta_meta_tokens': 'delta_w', 'delta_norm_w': 'delta_w', 'delta_w_in': 'delta_w', 'delta_sb_q_norm': 'delta_w', 'delta_sb_k_norm': 'delta_w', 'delta_gdn_conv_w': 'delta_w', 'delta_gdn_a_log': 'delta_w', 'delta_gdn_dt_bias': 'delta_w', 'delta_gdn_out_norm': 'delta_w', 'delta_hgrn_lb_logits': 'delta_w', 'delta_hgrn_out_norm': 'delta_w', 'delta_w_branch': 'delta_w', 'delta_w_out': 'delta_w', 'new_m_meta_tokens': 'new_m', 'new_m_norm_w': 'new_m', 'new_m_w_in': 'new_m', 'new_m_sb_q_norm': 'new_m', 'new_m_sb_k_norm': 'new_m', 'new_m_gdn_conv_w': 'new_m', 'new_m_gdn_a_log': 'new_m', 'new_m_gdn_dt_bias': 'new_m', 'new_m_gdn_out_norm': 'new_m', 'new_m_hgrn_lb_logits': 'new_m', 'new_m_hgrn_out_norm': 'new_m', 'new_m_w_branch': 'new_m', 'new_m_w_out': 'new_m', 'new_v_meta_tokens': 'new_v', 'new_v_norm_w': 'new_v', 'new_v_w_in': 'new_v', 'new_v_sb_q_norm': 'new_v', 'new_v_sb_k_norm': 'new_v', 'new_v_gdn_conv_w': 'new_v', 'new_v_gdn_a_log': 'new_v', 'new_v_gdn_dt_bias': 'new_v', 'new_v_gdn_out_norm': 'new_v', 'new_v_hgrn_lb_logits': 'new_v', 'new_v_hgrn_out_norm': 'new_v', 'new_v_w_branch': 'new_v', 'new_v_w_out': 'new_v'}


def _forward(args):
    return _fwd_reference(*[args[k] for k in FWD_PARAMS])


def _output_shape():
    def fwd():
        inp = _fwd_setup_inputs(0)
        return _fwd_reference(*[inp[k] for k in FWD_PARAMS])
    out = _jax.eval_shape(fwd)
    return out.shape, out.dtype

N_MICROBATCH = 1
ADAM_LR = 0.001
ADAM_B1 = 0.9
ADAM_B2 = 0.999
ADAM_EPS = 1e-08
ADAM_WD = 0.01
ADAM_STEP = 10
PER_EXAMPLE_BATCH_AXIS = {'x': 0, 'loss_target': 0}
SHARED_INPUTS = []
_WEIGHT_DTYPES = {'meta_tokens': _jnp.float32, 'norm_w': _jnp.float32, 'w_in': _jnp.float32, 'sb_q_norm': _jnp.float32, 'sb_k_norm': _jnp.float32, 'gdn_conv_w': _jnp.float32, 'gdn_a_log': _jnp.float32, 'gdn_dt_bias': _jnp.float32, 'gdn_out_norm': _jnp.float32, 'hgrn_lb_logits': _jnp.float32, 'hgrn_out_norm': _jnp.float32, 'w_branch': _jnp.float32, 'w_out': _jnp.float32}
MOMENT_SCALE = {'meta_tokens': 3.791575e-02, 'norm_w': 2.351421e+01, 'w_in': 3.636684e-01, 'sb_q_norm': 3.197782e+00, 'sb_k_norm': 3.193876e+00, 'gdn_conv_w': 8.687943e-01, 'gdn_a_log': 2.634624e+01, 'gdn_dt_bias': 2.483316e+01, 'gdn_out_norm': 5.927990e+01, 'hgrn_lb_logits': 2.633075e-02, 'hgrn_out_norm': 5.335256e+01, 'w_branch': 7.293989e-01, 'w_out': 1.236479e+00}


def _to_microbatches(a, axis):
    t = _jnp.moveaxis(a, axis, 0)
    t = t.reshape((N_MICROBATCH, t.shape[0] // N_MICROBATCH) + t.shape[1:])
    return _jnp.moveaxis(t, 1, axis + 1)


def setup_inputs(seed: int = 0) -> dict:
    inp = _fwd_setup_inputs(seed)
    key = _jax.random.fold_in(_jax.random.key(seed), 7919)
    shape, _ = _output_shape()
    out = dict(inp)
    out["loss_target"] = _jax.random.normal(_jax.random.fold_in(key, 0), shape, _jnp.float32)
    for i, name in enumerate(TWIN_WEIGHTS):
        w = inp[name].astype(_jnp.float32)
        if MOMENT_SCALE is None:
            s = _jnp.sqrt(_jnp.mean(_jnp.square(w)) + 1e-30)
        else:
            s = MOMENT_SCALE[name]
        km, kv = _jax.random.split(_jax.random.fold_in(key, i + 1))
        out[name] = w
        out["m_" + name] = s * _jax.random.normal(km, w.shape, _jnp.float32)
        out["v_" + name] = (s * s) * _jax.random.uniform(kv, w.shape, _jnp.float32, 0.5, 1.5)
    if N_MICROBATCH > 1:
        for name, axis in PER_EXAMPLE_BATCH_AXIS.items():
            out[name] = _to_microbatches(out[name], axis)
    return {'x': out['x'], 'meta_tokens': out['meta_tokens'], 'norm_w': out['norm_w'], 'w_in': out['w_in'], 'sb_q_norm': out['sb_q_norm'], 'sb_k_norm': out['sb_k_norm'], 'gdn_conv_w': out['gdn_conv_w'], 'gdn_a_log': out['gdn_a_log'], 'gdn_dt_bias': out['gdn_dt_bias'], 'gdn_out_norm': out['gdn_out_norm'], 'hgrn_lb_logits': out['hgrn_lb_logits'], 'hgrn_out_norm': out['hgrn_out_norm'], 'w_branch': out['w_branch'], 'w_out': out['w_out'], 'loss_target': out['loss_target'], 'm_meta_tokens': out['m_meta_tokens'], 'm_norm_w': out['m_norm_w'], 'm_w_in': out['m_w_in'], 'm_sb_q_norm': out['m_sb_q_norm'], 'm_sb_k_norm': out['m_sb_k_norm'], 'm_gdn_conv_w': out['m_gdn_conv_w'], 'm_gdn_a_log': out['m_gdn_a_log'], 'm_gdn_dt_bias': out['m_gdn_dt_bias'], 'm_gdn_out_norm': out['m_gdn_out_norm'], 'm_hgrn_lb_logits': out['m_hgrn_lb_logits'], 'm_hgrn_out_norm': out['m_hgrn_out_norm'], 'm_w_branch': out['m_w_branch'], 'm_w_out': out['m_w_out'], 'v_meta_tokens': out['v_meta_tokens'], 'v_norm_w': out['v_norm_w'], 'v_w_in': out['v_w_in'], 'v_sb_q_norm': out['v_sb_q_norm'], 'v_sb_k_norm': out['v_sb_k_norm'], 'v_gdn_conv_w': out['v_gdn_conv_w'], 'v_gdn_a_log': out['v_gdn_a_log'], 'v_gdn_dt_bias': out['v_gdn_dt_bias'], 'v_gdn_out_norm': out['v_gdn_out_norm'], 'v_hgrn_lb_logits': out['v_hgrn_lb_logits'], 'v_hgrn_out_norm': out['v_hgrn_out_norm'], 'v_w_branch': out['v_w_branch'], 'v_w_out': out['v_w_out']}


def _loss(weights, diff, rest, loss_target):
    with _jax.named_scope("forward"):
        args = {**rest, TWIN_DIFF_INPUT: diff, **{k: w.astype(_WEIGHT_DTYPES[k]) for k, w in weights.items()}}
        y = _forward(args)
    with _jax.named_scope("loss_head"):
        err = _jnp.square(y.astype(_jnp.float32) - loss_target)
        return 0.5 * _jnp.sum(_jnp.mean(err, axis=-1)) if err.ndim else 0.5 * err


def _adamw(w, g, m, v):
    m = ADAM_B1 * m + (1.0 - ADAM_B1) * g
    v = ADAM_B2 * v + (1.0 - ADAM_B2) * _jnp.square(g)
    m_hat = m / (1.0 - ADAM_B1 ** ADAM_STEP)
    v_hat = v / (1.0 - ADAM_B2 ** ADAM_STEP)
    delta = -ADAM_LR * (m_hat / (_jnp.sqrt(v_hat) + ADAM_EPS) + ADAM_WD * w)
    return delta, m, v


def reference(x, meta_tokens, norm_w, w_in, sb_q_norm, sb_k_norm, gdn_conv_w, gdn_a_log, gdn_dt_bias, gdn_out_norm, hgrn_lb_logits, hgrn_out_norm, w_branch, w_out, loss_target, m_meta_tokens, m_norm_w, m_w_in, m_sb_q_norm, m_sb_k_norm, m_gdn_conv_w, m_gdn_a_log, m_gdn_dt_bias, m_gdn_out_norm, m_hgrn_lb_logits, m_hgrn_out_norm, m_w_branch, m_w_out, v_meta_tokens, v_norm_w, v_w_in, v_sb_q_norm, v_sb_k_norm, v_gdn_conv_w, v_gdn_a_log, v_gdn_dt_bias, v_gdn_out_norm, v_hgrn_lb_logits, v_hgrn_out_norm, v_w_branch, v_w_out):
    given = dict(x=x, meta_tokens=meta_tokens, norm_w=norm_w, w_in=w_in, sb_q_norm=sb_q_norm, sb_k_norm=sb_k_norm, gdn_conv_w=gdn_conv_w, gdn_a_log=gdn_a_log, gdn_dt_bias=gdn_dt_bias, gdn_out_norm=gdn_out_norm, hgrn_lb_logits=hgrn_lb_logits, hgrn_out_norm=hgrn_out_norm, w_branch=w_branch, w_out=w_out, loss_target=loss_target, m_meta_tokens=m_meta_tokens, m_norm_w=m_norm_w, m_w_in=m_w_in, m_sb_q_norm=m_sb_q_norm, m_sb_k_norm=m_sb_k_norm, m_gdn_conv_w=m_gdn_conv_w, m_gdn_a_log=m_gdn_a_log, m_gdn_dt_bias=m_gdn_dt_bias, m_gdn_out_norm=m_gdn_out_norm, m_hgrn_lb_logits=m_hgrn_lb_logits, m_hgrn_out_norm=m_hgrn_out_norm, m_w_branch=m_w_branch, m_w_out=m_w_out, v_meta_tokens=v_meta_tokens, v_norm_w=v_norm_w, v_w_in=v_w_in, v_sb_q_norm=v_sb_q_norm, v_sb_k_norm=v_sb_k_norm, v_gdn_conv_w=v_gdn_conv_w, v_gdn_a_log=v_gdn_a_log, v_gdn_dt_bias=v_gdn_dt_bias, v_gdn_out_norm=v_gdn_out_norm, v_hgrn_lb_logits=v_hgrn_lb_logits, v_hgrn_out_norm=v_hgrn_out_norm, v_w_branch=v_w_branch, v_w_out=v_w_out)
    weights = {n: given[n] for n in TWIN_WEIGHTS}
    shared = {n: given[n] for n in SHARED_INPUTS}
    per_example = {n: given[n] for n in ['x']}
    grad_fn = _jax.value_and_grad(_loss, argnums=(0, 1))

    def one_microbatch(ex, loss_target):
        ex = dict(ex)
        diff = ex.pop(TWIN_DIFF_INPUT)
        return grad_fn(weights, diff, {**shared, **ex}, loss_target)

    if N_MICROBATCH == 1:
        loss, (grad_w, grad_x) = one_microbatch(per_example, given["loss_target"])
    else:
        def body(carry, xs):
            loss_sum, grad_sum = carry
            l_k, (gw_k, gx_k) = one_microbatch(xs[0], xs[1])
            with _jax.named_scope("update"):
                return (loss_sum + l_k, _jax.tree.map(_jnp.add, grad_sum, gw_k)), gx_k

        init = (_jnp.zeros((), _jnp.float32), _jax.tree.map(_jnp.zeros_like, weights))
        (loss, grad_w), grad_x = _jax.lax.scan(body, init, (per_example, given["loss_target"]))
    with _jax.named_scope("update"):
        delta_w, new_m, new_v = {}, {}, {}
        for n in TWIN_WEIGHTS:
            delta_w[n], new_m[n], new_v[n] = _adamw(weights[n], grad_w[n], given["m_" + n], given["v_" + n])
    return (loss, grad_x, *[grad_w[n] for n in TWIN_WEIGHTS], *[delta_w[n] for n in TWIN_WEIGHTS],
            *[new_m[n] for n in TWIN_WEIGHTS], *[new_v[n] for n in TWIN_WEIGHTS])
```

```python
import functools
import math

import jax
import jax.numpy as jnp
from jax import lax
from jax.experimental import pallas as pl
from jax.experimental.pallas import tpu as pltpu

F32, BF16 = jnp.float32, jnp.bfloat16
HIGHEST = lax.Precision.HIGHEST
MESH = pl.DeviceIdType.MESH

D_MODEL = 1024
BRANCH = 512
HEAD = 128
N_HEADS = 4
N_BRANCHES = 3
CHUNK = 64
N_META = 16
FRONT = 128
PAD_FRONT = FRONT - N_META
EPS = 1e-6
SB_KEY_BLOCK = 128
HALO = 8
ROW_TILE = 320
SB_Q_TILE = 320
LOSS_TILE = 128
VMEM_LIMIT = 56 * 1024 * 1024

ADAM_LR, ADAM_B1, ADAM_B2, ADAM_EPS, ADAM_WD, ADAM_STEP = 0.001, 0.9, 0.999, 1e-08, 0.01, 10

_C_SB_QKV = (0, 1536)
_C_SB_Z = (1536, 2048)
_C_GD_QKV = (2048, 3584)
_C_GD_Z = (3584, 4096)
_C_GD_BA = (4096, 4104)
_C_HG_QFI = (4104, 5640)
_C_HG_Z = (5640, 6152)
_C_MIX = (6152, 9224)
N_IN = 9224


def _mxu(a):
    return a.astype(BF16)


def _dg(a, b, ca, cb, hi=False):
    dn = (((ca,), (cb,)), ((), ()))
    if hi:
        return lax.dot_general(a, b, dn, precision=HIGHEST, preferred_element_type=F32)
    return lax.dot_general(_mxu(a), _mxu(b), dn, preferred_element_type=F32)


def _make_mm(hi):
    @jax.custom_vjp
    def nn(a, b):
        return _dg(a, b, 1, 0, hi)

    @jax.custom_vjp
    def nt(a, b):
        return _dg(a, b, 1, 1, hi)

    @jax.custom_vjp
    def tn(a, b):
        return _dg(a, b, 0, 0, hi)

    nn.defvjp(lambda a, b: (nn(a, b), (a, b)), lambda r, g: (nt(g, r[1]), tn(r[0], g)))
    nt.defvjp(lambda a, b: (nt(a, b), (a, b)), lambda r, g: (nn(g, r[1]), tn(g, r[0])))
    tn.defvjp(lambda a, b: (tn(a, b), (a, b)), lambda r, g: (nt(r[1], g), nn(r[0], g)))
    return nn, nt, tn


mm_nn, mm_nt, mm_tn = _make_mm(False)
mh_nn, mh_nt, mh_tn = _make_mm(True)


@jax.custom_vjp
def _mm_w(a, w, wz):
    return _dg(a, w, 1, 0)


def _mm_w_fwd(a, w, wz):
    return _dg(a, w, 1, 0), (a, w)


def _mm_w_bwd(res, g):
    a, w = res
    return _dg(g, w, 1, 1), jnp.zeros_like(w), _dg(a, g, 0, 0)


_mm_w.defvjp(_mm_w_fwd, _mm_w_bwd)


def mmw(a, w, wz):
    return _dg(a, w, 1, 0) if wz is None else _mm_w(a, w, wz)


@functools.partial(jax.custom_vjp, nondiff_argnums=(1,))
def roll_rows(x, s):
    return pltpu.roll(x, s, 0)


roll_rows.defvjp(lambda x, s: (pltpu.roll(x, s, 0), None),
                 lambda s, r, g: (pltpu.roll(g, g.shape[0] - s, 0),))


def _sigmoid(x):
    return 0.5 * (jnp.tanh(0.5 * x) + 1.0)


def _silu(x):
    return x * _sigmoid(x)


def _softplus(x):
    return jnp.maximum(x, 0.0) + jnp.log(1.0 + jnp.exp(-jnp.abs(x)))


def _rms(x, w):
    return x * lax.rsqrt(jnp.mean(x * x, axis=-1, keepdims=True) + EPS) * w


def _l2(x):
    return x * lax.rsqrt(jnp.sum(x * x, axis=-1, keepdims=True) + EPS)


def _heads(x, fn):
    return jnp.concatenate([fn(x[:, h * HEAD:(h + 1) * HEAD]) for h in range(N_HEADS)], axis=-1)


def _valid_rows(row0, n):
    rows = row0 + lax.broadcasted_iota(jnp.int32, (n, 1), 0)
    return (rows >= PAD_FRONT).astype(F32)


def _wz(wzs, i):
    return None if wzs is None else wzs[i]


def f_sb_pre(vals, p, w, wzs, row0):
    (h,), (norm_w, qn, kn) = vals, p
    raw = mmw(_rms(h, norm_w), w[0], _wz(wzs, 0))
    q = _heads(raw[:, :BRANCH], lambda t: _rms(t, qn))
    k = _heads(raw[:, BRANCH:2 * BRANCH], lambda t: _rms(t, kn))
    return [q, k, raw[:, 2 * BRANCH:]]


def f_gd_pre(vals, p, w, wzs, row0):
    (h_ext,), (norm_w, conv_w, a_log, dt_bias) = vals, p
    tm = h_ext.shape[0] - HALO
    raw = mmw(_rms(h_ext, norm_w), w[0], _wz(wzs, 0))
    x = raw[:, :3 * BRANCH]
    y = conv_w[3:4] * x
    for i in range(3):
        y = y + conv_w[i:i + 1] * roll_rows(x, 3 - i)
    y = _silu(y[HALO:])
    gq = _heads(y[:, :BRANCH], _l2) * (HEAD ** -0.5)
    gk = _heads(y[:, BRANCH:2 * BRANCH], _l2)
    gv = y[:, 2 * BRANCH:]
    slab = raw[HALO:, 3 * BRANCH:]
    lane = lax.broadcasted_iota(jnp.int32, (1, HEAD), 1)
    beta = _sigmoid(slab) * _valid_rows(row0, tm)
    g = -jnp.exp(a_log) * _softplus(slab + dt_bias)
    bg = jnp.where(lane < N_HEADS, beta, jnp.where(lane < 2 * N_HEADS, g, 0.0))
    return [gq, gk, gv, bg]


def f_hg_pre(vals, p, w, wzs, row0):
    (h,), (norm_w, lb) = vals, p
    raw = mmw(_rms(h, norm_w), w[0], _wz(wzs, 0))
    hq = _silu(raw[:, :BRANCH])
    fp = raw[:, BRANCH:2 * BRANCH]
    forget = lb + (1.0 - lb) * _sigmoid(fp)
    hk = (1.0 - lb) * _sigmoid(-fp)
    hv = raw[:, 2 * BRANCH:] * _valid_rows(row0, h.shape[0])
    return [hq, hk, hv, jnp.log(forget)]


def _merge_u(h, o, norm_w, out_norm, w, wzs, normed):
    xn = _rms(h, norm_w)
    z = mmw(xn, w[0], _wz(wzs, 0))
    mix = mmw(xn, w[1], _wz(wzs, 1))
    if normed:
        o = _heads(o, lambda t: _rms(t, out_norm))
    return _sigmoid(mix) * mmw(o * _silu(z), w[2], _wz(wzs, 2))


def make_f_merge(normed, with_prev):
    def f(vals, p, w, wzs, row0):
        u = _merge_u(vals[0], vals[1], p[0], p[1], w, wzs, normed)
        return [vals[2] + u] if with_prev else [u]
    return f


def f_out(vals, p, w, wzs, row0):
    h, y = vals
    return [h + mmw(y, w[0], _wz(wzs, 0))]


def _full_spec(a):
    nd = a.ndim
    return pl.BlockSpec(a.shape, lambda i, _nd=nd: (0,) * _nd)


def stage_fwd(name, fn, rows, params, weights, out_widths, *, halo=False, tm=ROW_TILE):
    t_len = rows[0].shape[0]
    n = t_len // tm
    nr, npar, nw = len(rows), len(params), len(weights)

    def body(*refs):
        i = pl.program_id(0)
        refs = list(refs)
        prev_ref = refs.pop(0) if halo else None
        row_refs, refs = refs[:nr], refs[nr:]
        par_refs, refs = refs[:npar], refs[npar:]
        w_refs, out_refs = refs[:nw], refs[nw:]
        vals = [r[...] for r in row_refs]
        if halo:
            prev = jnp.where(i == 0, 0.0, prev_ref[...])
            vals[0] = jnp.concatenate([prev, vals[0]], axis=0)
        outs = fn(vals, [r[...] for r in par_refs], [r[...] for r in w_refs], None, i * tm)
        for o_ref, o in zip(out_refs, outs):
            o_ref[...] = o

    in_specs, args = [], []
    if halo:
        in_specs.append(pl.BlockSpec((HALO, rows[0].shape[1]),
                                     lambda i: (jnp.maximum(i * (tm // HALO) - 1, 0), 0)))
        args.append(rows[0])
    for r in rows:
        in_specs.append(pl.BlockSpec((tm, r.shape[1]), lambda i: (i, 0)))
        args.append(r)
    for a in list(params) + list(weights):
        in_specs.append(_full_spec(a))
        args.append(a)
    return pl.pallas_call(
        body, grid=(n,), in_specs=in_specs,
        out_specs=[pl.BlockSpec((tm, wd), lambda i: (i, 0)) for wd in out_widths],
        out_shape=[jax.ShapeDtypeStruct((t_len, wd), F32) for wd in out_widths],
        compiler_params=pltpu.CompilerParams(dimension_semantics=("arbitrary",), vmem_limit_bytes=VMEM_LIMIT),
        name=name,
    )(*args)


def stage_vjp(name, fn, rows, params, weights, cts, diff_rows, *, halo=False, acc_in=None, tm=ROW_TILE):
    t_len = rows[0].shape[0]
    n = t_len // tm
    nr, npar, nw, nct, nd = len(rows), len(params), len(weights), len(cts), len(diff_rows)
    has_acc = acc_in is not None

    def body(*refs):
        i = pl.program_id(0)
        tile = n - 1 - i
        refs = list(refs)
        prev_ref = refs.pop(0) if halo else None
        row_refs, refs = refs[:nr], refs[nr:]
        par_refs, refs = refs[:npar], refs[npar:]
        w_refs, refs = refs[:nw], refs[nw:]
        ct_refs, refs = refs[:nct], refs[nct:]
        acc_ref = refs.pop(0) if has_acc else None
        drow_refs, refs = refs[:nd], refs[nd:]
        dpar_refs, refs = refs[:npar], refs[npar:]
        dw_refs, refs = refs[:nw], refs[nw:]
        carry_ref = refs[0] if halo else None

        vals = [r[...] for r in row_refs]
        if halo:
            prev = jnp.where(tile == 0, 0.0, prev_ref[...])
            vals[0] = jnp.concatenate([prev, vals[0]], axis=0)
        pvals = [r[...] for r in par_refs]
        wvals = [r[...] for r in w_refs]
        wzs = [jnp.zeros(w.shape, F32) for w in wvals]

        def f(dvals, pv, wz):
            full = list(vals)
            for k, idx in enumerate(diff_rows):
                full[idx] = dvals[k]
            return fn(full, pv, wvals, wz, tile * tm)

        _, vjp = jax.vjp(f, [vals[k] for k in diff_rows], pvals, wzs)
        d_rows, d_par, d_w = vjp([r[...] for r in ct_refs])

        @pl.when(i == 0)
        def _():
            for r in list(dpar_refs) + list(dw_refs):
                r[...] = jnp.zeros(r.shape, F32)

        for r, g in zip(list(dpar_refs) + list(dw_refs), list(d_par) + list(d_w)):
            r[...] += g
        for k, (r, g) in enumerate(zip(drow_refs, d_rows)):
            if k == 0 and diff_rows[0] == 0:
                if halo:
                    g_ext = g
                    g = g_ext[HALO:]
                    tail = g[tm - HALO:] + jnp.where(i == 0, 0.0, carry_ref[...])
                    g = jnp.concatenate([g[:tm - HALO], tail], axis=0)
                    carry_ref[...] = g_ext[:HALO]
                if has_acc:
                    g = g + acc_ref[...]
            r[...] = g

    rev = lambda i: (n - 1 - i, 0)
    in_specs, args = [], []
    if halo:
        in_specs.append(pl.BlockSpec((HALO, rows[0].shape[1]),
                                     lambda i: (jnp.maximum((n - 1 - i) * (tm // HALO) - 1, 0), 0)))
        args.append(rows[0])
    for r in list(rows):
        in_specs.append(pl.BlockSpec((tm, r.shape[1]), rev))
        args.append(r)
    for a in list(params) + list(weights):
        in_specs.append(_full_spec(a))
        args.append(a)
    for c in cts:
        in_specs.append(pl.BlockSpec((tm, c.shape[1]), rev))
        args.append(c)
    if has_acc:
        in_specs.append(pl.BlockSpec((tm, acc_in.shape[1]), rev))
        args.append(acc_in)
    out_specs = [pl.BlockSpec((tm, rows[k].shape[1]), rev) for k in diff_rows]
    out_shape = [jax.ShapeDtypeStruct(rows[k].shape, F32) for k in diff_rows]
    for a in list(params) + list(weights):
        out_specs.append(_full_spec(a))
        out_shape.append(jax.ShapeDtypeStruct(a.shape, F32))
    scratch = [pltpu.VMEM((HALO, rows[0].shape[1]), F32)] if halo else []
    outs = pl.pallas_call(
        body, grid=(n,), in_specs=in_specs, out_specs=out_specs, out_shape=out_shape, scratch_shapes=scratch,
        compiler_params=pltpu.CompilerParams(dimension_semantics=("arbitrary",), vmem_limit_bytes=VMEM_LIMIT),
        name=name,
    )(*args)
    return outs[:nd], outs[nd:nd + npar], outs[nd + npar:]


def _sb_tile(qb, kj, t_idx, ks):
    z = _dg(qb, kj, 1, 1) * (HEAD ** -0.5)
    s_idx = ks + lax.broadcasted_iota(jnp.int32, (1, SB_KEY_BLOCK), 1)
    mask = (s_idx < t_idx) & (s_idx >= PAD_FRONT)
    sp = jnp.log(1.0 + jnp.exp(-jnp.abs(z)))
    ls = jnp.minimum(z, 0.0) - sp
    lk = jnp.where(mask, ls - z, 0.0)
    return mask, ls, lk


def _tri(n, kind):
    r = lax.broadcasted_iota(jnp.int32, (n, n), 0)
    c = lax.broadcasted_iota(jnp.int32, (n, n), 1)
    return {"gt": r > c, "ge": r >= c, "eq": r == c}[kind]


def sb_fwd(q, k, v, *, tq=SB_Q_TILE):
    t_len = q.shape[0]
    nq = t_len // tq
    kb = SB_KEY_BLOCK

    def body(q_ref, k_ref, v_ref, o_ref):
        i = pl.program_id(1)
        qb = _mxu(q_ref[...])
        t_idx = i * tq + lax.broadcasted_iota(jnp.int32, (tq, 1), 0)
        after = _tri(kb, "gt").astype(F32)
        o_ref[...] = jnp.zeros(o_ref.shape, F32)
        n_blocks = ((i + 1) * tq + kb - 1) // kb

        def step(jj, run):
            ks = pl.multiple_of((n_blocks - 1 - jj) * kb, kb)
            kj = _mxu(k_ref[pl.ds(ks, kb), :])
            vj = _mxu(v_ref[pl.ds(ks, kb), :])
            mask, ls, lk = _sb_tile(qb, kj, t_idx, ks)
            inner = _dg(lk, after, 1, 0, hi=True)
            a = jnp.where(mask, jnp.exp(ls + run + inner), 0.0)
            o_ref[...] += _dg(a, vj, 1, 0)
            return run + inner[:, 0:1] + lk[:, 0:1]

        lax.fori_loop(0, n_blocks, step, jnp.zeros((tq, 1), F32))

    col = pl.BlockSpec((t_len, HEAD), lambda h, i: (0, h))
    tile = pl.BlockSpec((tq, HEAD), lambda h, i: (i, h))
    return pl.pallas_call(
        body, grid=(N_HEADS, nq), in_specs=[tile, col, col], out_specs=tile,
        out_shape=jax.ShapeDtypeStruct((t_len, BRANCH), F32),
        compiler_params=pltpu.CompilerParams(dimension_semantics=("arbitrary", "arbitrary"),
                                             vmem_limit_bytes=VMEM_LIMIT),
        name="sb_fwd",
    )(q, k, v)


def sb_bwd(q, k, v, o, do, *, tq=SB_Q_TILE):
    t_len = q.shape[0]
    nq = t_len // tq
    kb = SB_KEY_BLOCK
    scale = HEAD ** -0.5

    def body(q_ref, k_ref, v_ref, o_ref, do_ref, dq_ref, dk_ref, dv_ref):
        i = pl.program_id(1)

        @pl.when(i == 0)
        def _():
            dk_ref[...] = jnp.zeros(dk_ref.shape, F32)
            dv_ref[...] = jnp.zeros(dv_ref.shape, F32)

        qb = _mxu(q_ref[...])
        do_f = do_ref[...]
        dob = _mxu(do_f)
        total = jnp.sum(dob.astype(F32) * o_ref[...], axis=-1, keepdims=True)
        t_idx = i * tq + lax.broadcasted_iota(jnp.int32, (tq, 1), 0)
        after = _tri(kb, "gt").astype(F32)
        from_here = _tri(kb, "ge").astype(F32)
        dq_ref[...] = jnp.zeros(dq_ref.shape, F32)
        n_blocks = ((i + 1) * tq + kb - 1) // kb

        def step(jj, carry):
            run, run_dl = carry
            ks = pl.multiple_of((n_blocks - 1 - jj) * kb, kb)
            kj = _mxu(k_ref[pl.ds(ks, kb), :])
            vj = _mxu(v_ref[pl.ds(ks, kb), :])
            mask, ls, lk = _sb_tile(qb, kj, t_idx, ks)
            inner = _dg(lk, after, 1, 0, hi=True)
            a = jnp.where(mask, jnp.exp(ls + run + inner), 0.0)
            ab = _mxu(a)
            dl = ab.astype(F32) * _dg(dob, vj, 1, 1)
            suffix = _dg(dl, from_here, 1, 0, hi=True)
            before = total - (run_dl + suffix)
            sig = jnp.exp(ls)
            dz = (dl * (1.0 - sig) - jnp.where(mask, sig * before, 0.0)) * scale
            dzb = _mxu(dz)
            dq_ref[...] += _dg(dzb, kj, 1, 0)
            dk_ref[pl.ds(ks, kb), :] += _dg(dzb, qb, 0, 0)
            dv_ref[pl.ds(ks, kb), :] += _dg(ab, dob, 0, 0)
            return run + inner[:, 0:1] + lk[:, 0:1], run_dl + suffix[:, 0:1]

        zero = jnp.zeros((tq, 1), F32)
        lax.fori_loop(0, n_blocks, step, (zero, zero))

    col = pl.BlockSpec((t_len, HEAD), lambda h, i: (0, h))
    tile = pl.BlockSpec((tq, HEAD), lambda h, i: (i, h))
    full = jax.ShapeDtypeStruct((t_len, BRANCH), F32)
    return pl.pallas_call(
        body, grid=(N_HEADS, nq), in_specs=[tile, col, col, tile, tile], out_specs=[tile, col, col],
        out_shape=[full, full, full],
        compiler_params=pltpu.CompilerParams(dimension_semantics=("arbitrary", "arbitrary"),
                                             vmem_limit_bytes=VMEM_LIMIT),
        name="sb_bwd",
    )(q, k, v, o, do)


def gdn_chunk(state, q, k, v, bg, head):
    c = CHUNK
    beta = bg[:, head:head + 1]
    g = bg[:, N_HEADS + head:N_HEADS + head + 1]
    causal, strict = _tri(c, "ge"), _tri(c, "gt")
    eye = _tri(c, "eq").astype(F32)
    g_lanes = mh_nn(causal.astype(F32), g * jnp.ones((1, HEAD), F32))
    g_col = g_lanes[:, :c]
    g_row = mh_nn(jnp.ones((c, c), F32), eye * g_col)
    decay = jnp.where(causal, jnp.exp(jnp.where(causal, g_col - g_row, 0.0)), 0.0)
    kb = k * beta
    m = jnp.where(strict, mm_nt(kb, k) * decay, 0.0)
    inv = eye - m
    p = mh_nn(m, m)
    for _ in range(5):
        inv = inv + mh_nn(inv, p)
        p = mh_nn(p, p)
    g1 = g_lanes[:, 0:1]
    g_last = g1[c - 1:c, :]
    u = mh_nn(inv, v * beta)
    w = mh_nn(inv, kb * jnp.exp(g1))
    a_qk = jnp.where(causal, mm_nt(q, k) * decay, 0.0)
    v_new = u - mm_nn(w, state)
    o = mm_nn(q * jnp.exp(g1), state) + mm_nn(a_qk, v_new)
    state = state * jnp.exp(g_last) + mm_tn(k * jnp.exp(g_last - g1), v_new)
    return state, o


def gdn_fwd(q, k, v, bg, *, tg=ROW_TILE):
    t_len = q.shape[0]
    n, nc = t_len // tg, tg // CHUNK

    def body(q_ref, k_ref, v_ref, bg_ref, o_ref, s_all_ref, s_ref):
        @pl.when(pl.program_id(0) == 0)
        def _():
            s_ref[...] = jnp.zeros(s_ref.shape, F32)

        def step(c, _):
            r = pl.ds(pl.multiple_of(c * CHUNK, CHUNK), CHUNK)
            bgc = bg_ref[r, :]
            for h in range(N_HEADS):
                cs = slice(h * HEAD, (h + 1) * HEAD)
                s_in = s_ref[h]
                s_all_ref[c, h] = s_in
                s_out, o = gdn_chunk(s_in, q_ref[r, cs], k_ref[r, cs], v_ref[r, cs], bgc, h)
                o_ref[r, cs] = o
                s_ref[h] = s_out
            return 0

        lax.fori_loop(0, nc, step, 0)

    row = lambda wd: pl.BlockSpec((tg, wd), lambda i: (i, 0))
    return pl.pallas_call(
        body, grid=(n,), in_specs=[row(BRANCH), row(BRANCH), row(BRANCH), row(HEAD)],
        out_specs=[row(BRANCH), pl.BlockSpec((nc, N_HEADS, HEAD, HEAD), lambda i: (i, 0, 0, 0))],
        out_shape=[jax.ShapeDtypeStruct((t_len, BRANCH), F32),
                   jax.ShapeDtypeStruct((t_len // CHUNK, N_HEADS, HEAD, HEAD), F32)],
        scratch_shapes=[pltpu.VMEM((N_HEADS, HEAD, HEAD), F32)],
        compiler_params=pltpu.CompilerParams(dimension_semantics=("arbitrary",), vmem_limit_bytes=VMEM_LIMIT),
        name="gdn_fwd",
    )(q, k, v, bg)


def gdn_bwd(q, k, v, bg, s_all, do, *, tg=ROW_TILE):
    t_len = q.shape[0]
    n, nc = t_len // tg, tg // CHUNK

    def body(q_ref, k_ref, v_ref, bg_ref, s_all_ref, do_ref, dq_ref, dk_ref, dv_ref, dbg_ref, ds_ref):
        @pl.when(pl.program_id(0) == 0)
        def _():
            ds_ref[...] = jnp.zeros(ds_ref.shape, F32)

        def step(cc, _):
            c = nc - 1 - cc
            r = pl.ds(pl.multiple_of(c * CHUNK, CHUNK), CHUNK)
            bgc = bg_ref[r, :]
            dbg = jnp.zeros((CHUNK, HEAD), F32)
            for h in range(N_HEADS):
                cs = slice(h * HEAD, (h + 1) * HEAD)
                _, vjp = jax.vjp(functools.partial(gdn_chunk, head=h),
                                 s_all_ref[c, h], q_ref[r, cs], k_ref[r, cs], v_ref[r, cs], bgc)
                ds, dq, dk, dv, dbg_h = vjp((ds_ref[h], do_ref[r, cs]))
                ds_ref[h] = ds
                dq_ref[r, cs] = dq
                dk_ref[r, cs] = dk
                dv_ref[r, cs] = dv
                dbg = dbg + dbg_h
            dbg_ref[r, :] = dbg
            return 0

        lax.fori_loop(0, nc, step, 0)

    row = lambda wd: pl.BlockSpec((tg, wd), lambda i: (n - 1 - i, 0))
    wide, slab = jax.ShapeDtypeStruct((t_len, BRANCH), F32), jax.ShapeDtypeStruct((t_len, HEAD), F32)
    return pl.pallas_call(
        body, grid=(n,),
        in_specs=[row(BRANCH), row(BRANCH), row(BRANCH), row(HEAD),
                  pl.BlockSpec((nc, N_HEADS, HEAD, HEAD), lambda i: (n - 1 - i, 0, 0, 0)), row(BRANCH)],
        out_specs=[row(BRANCH), row(BRANCH), row(BRANCH), row(HEAD)],
        out_shape=[wide, wide, wide, slab],
        scratch_shapes=[pltpu.VMEM((N_HEADS, HEAD, HEAD), F32)],
        compiler_params=pltpu.CompilerParams(dimension_semantics=("arbitrary",), vmem_limit_bytes=VMEM_LIMIT),
        name="gdn_bwd",
    )(q, k, v, bg, s_all, do)


def hgrn_chunk(state_t, q, k, v, g):
    c = CHUNK
    gc = mh_nn(_tri(c, "ge").astype(F32), g)
    o = mm_nt(q * jnp.exp(gc), state_t)
    row = lax.broadcasted_iota(jnp.int32, (c, 1), 0)
    for d in range(c):
        kr, gr, vr = (k, gc, v) if d == 0 else (roll_rows(k, d), roll_rows(gc, d), roll_rows(v, d))
        ok = row >= d
        e = jnp.exp(jnp.where(ok, gc - gr, 0.0))
        a = jnp.where(ok, jnp.sum(q * kr * e, axis=-1, keepdims=True), 0.0)
        o = o + a * vr
    g_end = gc[c - 1:c, :]
    state_t = state_t * jnp.exp(g_end) + mm_tn(v, k * jnp.exp(g_end - gc))
    return state_t, o


def hgrn_fwd(q, k, v, g, *, tg=ROW_TILE):
    t_len = q.shape[0]
    n, nc = t_len // tg, tg // CHUNK

    def body(q_ref, k_ref, v_ref, g_ref, o_ref, s_all_ref, s_ref):
        @pl.when(pl.program_id(1) == 0)
        def _():
            s_ref[...] = jnp.zeros(s_ref.shape, F32)

        def step(c, _):
            r = pl.ds(pl.multiple_of(c * CHUNK, CHUNK), CHUNK)
            s_in = s_ref[...]
            s_all_ref[c, 0] = s_in
            s_out, o = hgrn_chunk(s_in, q_ref[r, :], k_ref[r, :], v_ref[r, :], g_ref[r, :])
            o_ref[r, :] = o
            s_ref[...] = s_out
            return 0

        lax.fori_loop(0, nc, step, 0)

    row = pl.BlockSpec((tg, HEAD), lambda h, i: (i, h))
    return pl.pallas_call(
        body, grid=(N_HEADS, n), in_specs=[row, row, row, row],
        out_specs=[row, pl.BlockSpec((nc, 1, HEAD, HEAD), lambda h, i: (i, h, 0, 0))],
        out_shape=[jax.ShapeDtypeStruct((t_len, BRANCH), F32),
                   jax.ShapeDtypeStruct((t_len // CHUNK, N_HEADS, HEAD, HEAD), F32)],
        scratch_shapes=[pltpu.VMEM((HEAD, HEAD), F32)],
        compiler_params=pltpu.CompilerParams(dimension_semantics=("arbitrary", "arbitrary"),
                                             vmem_limit_bytes=VMEM_LIMIT),
        name="hgrn_fwd",
    )(q, k, v, g)


def hgrn_bwd(q, k, v, g, s_all, do, *, tg=ROW_TILE):
    t_len = q.shape[0]
    n, nc = t_len // tg, tg // CHUNK

    def body(q_ref, k_ref, v_ref, g_ref, s_all_ref, do_ref, dq_ref, dk_ref, dv_ref, dg_ref, ds_ref):
        @pl.when(pl.program_id(1) == 0)
        def _():
            ds_ref[...] = jnp.zeros(ds_ref.shape, F32)

        def step(cc, _):
            c = nc - 1 - cc
            r = pl.ds(pl.multiple_of(c * CHUNK, CHUNK), CHUNK)
            _, vjp = jax.vjp(hgrn_chunk, s_all_ref[c, 0], q_ref[r, :], k_ref[r, :], v_ref[r, :], g_ref[r, :])
            ds, dq, dk, dv, dg = vjp((ds_ref[...], do_ref[r, :]))
            ds_ref[...] = ds
            dq_ref[r, :] = dq
            dk_ref[r, :] = dk
            dv_ref[r, :] = dv
            dg_ref[r, :] = dg
            return 0

        lax.fori_loop(0, nc, step, 0)

    row = pl.BlockSpec((tg, HEAD), lambda h, i: (n - 1 - i, h))
    wide = jax.ShapeDtypeStruct((t_len, BRANCH), F32)
    return pl.pallas_call(
        body, grid=(N_HEADS, n),
        in_specs=[row, row, row, row, pl.BlockSpec((nc, 1, HEAD, HEAD), lambda h, i: (n - 1 - i, h, 0, 0)), row],
        out_specs=[row, row, row, row], out_shape=[wide, wide, wide, wide],
        scratch_shapes=[pltpu.VMEM((HEAD, HEAD), F32)],
        compiler_params=pltpu.CompilerParams(dimension_semantics=("arbitrary", "arbitrary"),
                                             vmem_limit_bytes=VMEM_LIMIT),
        name="hgrn_bwd",
    )(q, k, v, g, s_all, do)


def loss_head(h, target, *, tl=LOSS_TILE):
    seq = target.shape[0]
    n = seq // tl
    off = FRONT // tl

    def body(h_ref, t_ref, sq_ref, dy_ref):
        @pl.when(pl.program_id(0) == 0)
        def _():
            sq_ref[...] = jnp.zeros(sq_ref.shape, F32)

        err = h_ref[...] - t_ref[...]
        sq_ref[...] += jnp.sum(err * err, keepdims=True)
        dy_ref[...] = err * (1.0 / D_MODEL)

    return pl.pallas_call(
        body, grid=(n,),
        in_specs=[pl.BlockSpec((tl, D_MODEL), lambda i: (i + off, 0)), pl.BlockSpec((tl, D_MODEL), lambda i: (i, 0))],
        out_specs=[pl.BlockSpec((1, 1), lambda i: (0, 0)), pl.BlockSpec((tl, D_MODEL), lambda i: (i, 0))],
        out_shape=[jax.ShapeDtypeStruct((1, 1), F32), jax.ShapeDtypeStruct((seq, D_MODEL), F32)],
        compiler_params=pltpu.CompilerParams(dimension_semantics=("arbitrary",)),
        name="loss_head",
    )(h, target)


def _pad_lanes(a, lo, n=HEAD):
    return jnp.pad(a.astype(F32), (lo, n - lo - a.shape[0])).reshape(1, n)


def _lower_bounds(logits):
    p = jax.nn.softmax(logits.astype(F32), axis=0)
    return jnp.cumsum(p, axis=0) - p[0:1]


def _layer_weights(w_in_l, w_branch_l, w_out_l):
    c = lambda r: w_in_l[:, r[0]:r[1]]
    ba = jnp.pad(c(_C_GD_BA), ((0, 0), (0, HEAD - 2 * N_HEADS)))
    mix = c(_C_MIX)
    return dict(
        sb=c(_C_SB_QKV), gd=jnp.concatenate([c(_C_GD_QKV), ba], axis=1), hg=c(_C_HG_QFI),
        z=[c(_C_SB_Z), c(_C_GD_Z), c(_C_HG_Z)],
        mix=[mix[:, b * D_MODEL:(b + 1) * D_MODEL] for b in range(N_BRANCHES)],
        br=[w_branch_l[b] for b in range(N_BRANCHES)], out=w_out_l)


def device_step(x, target, meta, norm_w, w_in, sb_qn, sb_kn, conv_w, a_log, dt_bias, gd_on, lb_logits, hg_on,
                w_branch, w_out):
    depth = norm_w.shape[0]
    lbs, lb_vjp = jax.vjp(_lower_bounds, lb_logits)
    h = jnp.concatenate([jnp.zeros((PAD_FRONT, D_MODEL), F32), meta, x], axis=0)
    row = lambda a: a.reshape(1, -1)
    saved = []
    for l in range(depth):
        w = _layer_weights(w_in[l], w_branch[l], w_out[l])
        nw = row(norm_w[l])
        p_sb = [nw, row(sb_qn[l]), row(sb_kn[l])]
        p_gd = [nw, conv_w[l], _pad_lanes(a_log[l], N_HEADS), _pad_lanes(dt_bias[l], N_HEADS)]
        p_hg = [nw, row(lbs[l])]
        out_norms = [row(sb_qn[l]), row(gd_on[l]), row(hg_on[l])]
        sq, sk, sv = stage_fwd("sb_pre", f_sb_pre, [h], p_sb, [w["sb"]], [BRANCH] * 3)
        o_sb = sb_fwd(sq, sk, sv)
        gq, gk, gv, bg = stage_fwd("gd_pre", f_gd_pre, [h], p_gd, [w["gd"]], [BRANCH] * 3 + [HEAD], halo=True)
        o_gd, gd_states = gdn_fwd(gq, gk, gv, bg)
        hq, hk, hv, hg = stage_fwd("hg_pre", f_hg_pre, [h], p_hg, [w["hg"]], [BRANCH] * 4)
        o_hg, hg_states = hgrn_fwd(hq, hk, hv, hg)
        branch_o = [o_sb, o_gd, o_hg]
        y = None
        for b in range(N_BRANCHES):
            rows = [h, branch_o[b]] + ([] if y is None else [y])
            (y,) = stage_fwd(f"merge{b}", make_f_merge(b > 0, y is not None), rows, [nw, out_norms[b]],
                             [w["z"][b], w["mix"][b], w["br"][b]], [D_MODEL])
        (h_next,) = stage_fwd("out_proj", f_out, [h, y], [], [w["out"]], [D_MODEL])
        saved.append(dict(h=h, w=w, p_sb=p_sb, p_gd=p_gd, p_hg=p_hg, out_norms=out_norms, y=y,
                          sb=(sq, sk, sv, o_sb), gd=(gq, gk, gv, bg, gd_states, o_gd),
                          hg=(hq, hk, hv, hg, hg_states, o_hg)))
        h = h_next

    sq_err, dy = loss_head(h, target)
    dh = jnp.concatenate([jnp.zeros((FRONT, D_MODEL), F32), dy], axis=0)

    g = dict(norm_w=[], w_in=[], sb_qn=[], sb_kn=[], conv_w=[], a_log=[], dt_bias=[], gd_on=[], lb=[], hg_on=[],
             w_branch=[], w_out=[])
    for l in reversed(range(depth)):
        s = saved[l]
        w, hl = s["w"], s["h"]
        nw = s["p_sb"][0]
        (dy_,), _, (d_wout,) = stage_vjp("out_proj_b", f_out, [hl, s["y"]], [], [w["out"]], [dh], [1])
        d_norm = jnp.zeros((1, D_MODEL), F32)
        d_o, d_on, d_wz, d_wmix, d_wbr = [None] * 3, [None] * 3, [None] * 3, [None] * 3, [None] * 3
        branch_o = [s["sb"][3], s["gd"][5], s["hg"][5]]
        for b in reversed(range(N_BRANCHES)):
            (dh, d_o[b]), (dn, d_on[b]), (d_wz[b], d_wmix[b], d_wbr[b]) = stage_vjp(
                f"merge{b}_b", make_f_merge(b > 0, False), [hl, branch_o[b]], [nw, s["out_norms"][b]],
                [w["z"][b], w["mix"][b], w["br"][b]], [dy_], [0, 1], acc_in=dh)
            d_norm = d_norm + dn
        hq, hk, hv, hg, hg_states, _ = s["hg"]
        d_hg = hgrn_bwd(hq, hk, hv, hg, hg_states, d_o[2])
        (dh,), (dn, d_lb), (d_whg,) = stage_vjp("hg_pre_b", f_hg_pre, [hl], s["p_hg"], [w["hg"]], list(d_hg), [0],
                                                acc_in=dh)
        d_norm = d_norm + dn
        gq, gk, gv, bg, gd_states, _ = s["gd"]
        d_gd = gdn_bwd(gq, gk, gv, bg, gd_states, d_o[1])
        (dh,), (dn, d_conv, d_alog, d_dtb), (d_wgd,) = stage_vjp(
            "gd_pre_b", f_gd_pre, [hl], s["p_gd"], [w["gd"]], list(d_gd), [0], halo=True, acc_in=dh)
        d_norm = d_norm + dn
        sq, sk, sv, o_sb = s["sb"]
        d_sb = sb_bwd(sq, sk, sv, o_sb, d_o[0])
        (dh,), (dn, d_qn, d_kn), (d_wsb,) = stage_vjp("sb_pre_b", f_sb_pre, [hl], s["p_sb"], [w["sb"]], list(d_sb),
                                                      [0], acc_in=dh)
        d_norm = d_norm + dn
        g["norm_w"].append(d_norm[0])
        g["w_in"].append(jnp.concatenate(
            [d_wsb, d_wz[0], d_wgd[:, :3 * BRANCH], d_wz[1], d_wgd[:, 3 * BRANCH:3 * BRANCH + 2 * N_HEADS],
             d_whg, d_wz[2]] + d_wmix, axis=1))
        g["sb_qn"].append(d_qn[0])
        g["sb_kn"].append(d_kn[0])
        g["conv_w"].append(d_conv)
        g["a_log"].append(d_alog[0, N_HEADS:2 * N_HEADS])
        g["dt_bias"].append(d_dtb[0, N_HEADS:2 * N_HEADS])
        g["gd_on"].append(d_on[1][0])
        g["hg_on"].append(d_on[2][0])
        g["lb"].append(d_lb[0])
        g["w_branch"].append(jnp.stack(d_wbr))
        g["w_out"].append(d_wout)
    g = {k: jnp.stack(v[::-1]) for k, v in g.items()}
    (g["lb_logits"],) = lb_vjp(g.pop("lb"))
    return sq_err, dh[FRONT:], dh[PAD_FRONT:FRONT], g


ANY = pl.BlockSpec(memory_space=pl.ANY)


def _coords():
    return lax.axis_index("x"), lax.axis_index("y"), lax.axis_index("c")


def _other_chips(x, y):
    return [(2 * px + py, (px, py)) for px, py in ((1 - x, y), (x, 1 - y), (1 - x, 1 - y))]


def gather_over_chips(shards):
    na = len(shards)

    def body(*refs):
        ins, outs = refs[:na], refs[na:2 * na]
        send_sems, recv_sems, local_sems = refs[2 * na:]
        x, y, c = _coords()
        me = 2 * x + y
        copies = []
        for a in range(na):
            own = pltpu.make_async_copy(ins[a], outs[a].at[me], local_sems.at[a])
            own.start()
            copies.append(own)
        sends = []
        for a in range(na):
            for k, (_, (px, py)) in enumerate(_other_chips(x, y)):
                cp = pltpu.make_async_remote_copy(
                    src_ref=ins[a], dst_ref=outs[a].at[me], send_sem=send_sems.at[a, k], recv_sem=recv_sems.at[a, k],
                    device_id=(px, py, c), device_id_type=MESH)
                cp.start()
                sends.append(cp)
        for a in range(na):
            for k, (p, (px, py)) in enumerate(_other_chips(x, y)):
                pltpu.make_async_remote_copy(
                    src_ref=ins[a], dst_ref=outs[a].at[p], send_sem=send_sems.at[a, k], recv_sem=recv_sems.at[a, k],
                    device_id=(px, py, c), device_id_type=MESH).wait_recv()
        for cp in sends:
            cp.wait_send()
        for cp in copies:
            cp.wait()

    return pl.pallas_call(
        body, in_specs=[ANY] * na, out_specs=[ANY] * na,
        out_shape=[jax.ShapeDtypeStruct((4,) + s.shape, s.dtype) for s in shards],
        scratch_shapes=[pltpu.SemaphoreType.DMA((na, 3)), pltpu.SemaphoreType.DMA((na, 3)),
                        pltpu.SemaphoreType.DMA((na,))],
        name="gather_over_chips",
    )(*shards)


def scatter_over_chips(parts):
    na = len(parts)

    def body(*refs):
        ins, outs = refs[:na], refs[na:2 * na]
        send_sems, recv_sems, local_sems = refs[2 * na:]
        x, y, c = _coords()
        me = 2 * x + y
        copies = []
        for a in range(na):
            own = pltpu.make_async_copy(ins[a].at[me], outs[a].at[me], local_sems.at[a])
            own.start()
            copies.append(own)
        sends = []
        for a in range(na):
            for k, (p, (px, py)) in enumerate(_other_chips(x, y)):
                cp = pltpu.make_async_remote_copy(
                    src_ref=ins[a].at[p], dst_ref=outs[a].at[me], send_sem=send_sems.at[a, k],
                    recv_sem=recv_sems.at[a, k], device_id=(px, py, c), device_id_type=MESH)
                cp.start()
                sends.append(cp)
        for a in range(na):
            for k, (p, (px, py)) in enumerate(_other_chips(x, y)):
                pltpu.make_async_remote_copy(
                    src_ref=ins[a].at[p], dst_ref=outs[a].at[p], send_sem=send_sems.at[a, k],
                    recv_sem=recv_sems.at[a, k], device_id=(px, py, c), device_id_type=MESH).wait_recv()
        for cp in sends:
            cp.wait_send()
        for cp in copies:
            cp.wait()

    return pl.pallas_call(
        body, in_specs=[ANY] * na, out_specs=[ANY] * na,
        out_shape=[jax.ShapeDtypeStruct(s.shape, s.dtype) for s in parts],
        scratch_shapes=[pltpu.SemaphoreType.DMA((na, 3)), pltpu.SemaphoreType.DMA((na, 3)),
                        pltpu.SemaphoreType.DMA((na,))],
        name="scatter_over_chips",
    )(*parts)


def swap_with_sibling(arrs):
    na = len(arrs)

    def body(*refs):
        ins, outs = refs[:na], refs[na:2 * na]
        send_sems, recv_sems = refs[2 * na:]
        x, y, c = _coords()
        cps = [pltpu.make_async_remote_copy(src_ref=ins[a], dst_ref=outs[a], send_sem=send_sems.at[a],
                                            recv_sem=recv_sems.at[a], device_id=(x, y, 1 - c), device_id_type=MESH)
               for a in range(na)]
        for cp in cps:
            cp.start()
        for cp in cps:
            cp.wait()

    return pl.pallas_call(
        body, in_specs=[ANY] * na, out_specs=[ANY] * na,
        out_shape=[jax.ShapeDtypeStruct(s.shape, s.dtype) for s in arrs],
        scratch_shapes=[pltpu.SemaphoreType.DMA((na,)), pltpu.SemaphoreType.DMA((na,))],
        name="swap_with_sibling",
    )(*arrs)


def gather_over_devices(block):
    def body(in_ref, out_ref, send_sems, recv_sems, local_sem):
        x, y, c = _coords()
        me = 4 * x + 2 * y + c
        own = pltpu.make_async_copy(in_ref, out_ref.at[me], local_sem)
        own.start()
        rel = [(dx, dy, dc) for dx in (0, 1) for dy in (0, 1) for dc in (0, 1)][1:]
        peers = [(1 - x if dx else x, 1 - y if dy else y, 1 - c if dc else c) for dx, dy, dc in rel]
        sends = []
        for k, peer in enumerate(peers):
            cp = pltpu.make_async_remote_copy(src_ref=in_ref, dst_ref=out_ref.at[me], send_sem=send_sems.at[k],
                                              recv_sem=recv_sems.at[k], device_id=peer, device_id_type=MESH)
            cp.start()
            sends.append(cp)
        for k, (px, py, pc) in enumerate(peers):
            pltpu.make_async_remote_copy(src_ref=in_ref, dst_ref=out_ref.at[4 * px + 2 * py + pc],
                                         send_sem=send_sems.at[k], recv_sem=recv_sems.at[k],
                                         device_id=(px, py, pc), device_id_type=MESH).wait_recv()
        for cp in sends:
            cp.wait_send()
        own.wait()

    return pl.pallas_call(
        body, in_specs=[ANY], out_specs=ANY, out_shape=jax.ShapeDtypeStruct((8,) + block.shape, block.dtype),
        scratch_shapes=[pltpu.SemaphoreType.DMA((7,)), pltpu.SemaphoreType.DMA((7,)), pltpu.SemaphoreType.DMA],
        name="gather_over_devices",
    )(block)


def _row_tile(rows, cols, n_streams):
    budget = 24 * 1024 * 1024 // (n_streams * 2 * 4 * max(cols, 128))
    best = None
    for t in range(8, rows + 1, 8):
        if rows % t == 0 and t <= budget:
            best = t
    return best if best is not None else rows


def sum_slabs(stack):
    k, rows, cols = stack.shape
    tr = _row_tile(rows, cols, k + 1)

    def body(in_ref, out_ref):
        acc = in_ref[0]
        for j in range(1, k):
            acc = acc + in_ref[j]
        out_ref[...] = acc

    return pl.pallas_call(
        body, grid=(rows // tr,), in_specs=[pl.BlockSpec((k, tr, cols), lambda i: (0, i, 0))],
        out_specs=pl.BlockSpec((tr, cols), lambda i: (i, 0)), out_shape=jax.ShapeDtypeStruct((rows, cols), F32),
        compiler_params=pltpu.CompilerParams(dimension_semantics=("arbitrary",), vmem_limit_bytes=VMEM_LIMIT),
        name="sum_slabs",
    )(stack)


def adamw(name, g_parts, w, m, v):
    rows, cols = w.shape
    k = len(g_parts)
    tr = _row_tile(rows, cols, k + 7)
    c1 = 1.0 - ADAM_B1 ** ADAM_STEP
    c2 = 1.0 - ADAM_B2 ** ADAM_STEP

    def body(*refs):
        g_refs, (w_ref, m_ref, v_ref, g_out, d_out, m_out, v_out) = refs[:k], refs[k:]
        g = g_refs[0][...]
        for r in g_refs[1:]:
            g = g + r[...]
        m_new = ADAM_B1 * m_ref[...] + (1.0 - ADAM_B1) * g
        v_new = ADAM_B2 * v_ref[...] + (1.0 - ADAM_B2) * (g * g)
        d_out[...] = -ADAM_LR * ((m_new / c1) / (jnp.sqrt(v_new / c2) + ADAM_EPS) + ADAM_WD * w_ref[...])
        g_out[...] = g
        m_out[...] = m_new
        v_out[...] = v_new

    spec = pl.BlockSpec((tr, cols), lambda i: (i, 0))
    shape = jax.ShapeDtypeStruct((rows, cols), F32)
    return pl.pallas_call(
        body, grid=(rows // tr,), in_specs=[spec] * (k + 3), out_specs=[spec] * 4, out_shape=[shape] * 4,
        compiler_params=pltpu.CompilerParams(dimension_semantics=("arbitrary",), vmem_limit_bytes=VMEM_LIMIT),
        name=name,
    )(*g_parts, w, m, v)


_SMALL = ("norm_w", "sb_q_norm", "sb_k_norm", "gdn_a_log", "gdn_dt_bias", "gdn_out_norm", "hgrn_lb_logits",
          "hgrn_out_norm")


def _pack(arrs, lead=()):
    flat = jnp.concatenate([jnp.reshape(a, (-1,)).astype(F32) for a in list(lead) + list(arrs)])
    n = flat.shape[0]
    rows = -(-n // (8 * 128)) * 8
    return jnp.pad(flat, (0, rows * 128 - n)).reshape(rows, 128)


def _unpack(packed, shapes, n_lead=0):
    flat = packed.reshape(-1)
    out, off = [], n_lead
    for s in shapes:
        n = math.prod(s)
        out.append(flat[off:off + n].reshape(s))
        off += n
    return out


def kernel(x, meta_tokens, norm_w, w_in, sb_q_norm, sb_k_norm, gdn_conv_w, gdn_a_log, gdn_dt_bias, gdn_out_norm, hgrn_lb_logits, hgrn_out_norm, w_branch, w_out, loss_target, m_meta_tokens, m_norm_w, m_w_in, m_sb_q_norm, m_sb_k_norm, m_gdn_conv_w, m_gdn_a_log, m_gdn_dt_bias, m_gdn_out_norm, m_hgrn_lb_logits, m_hgrn_out_norm, m_w_branch, m_w_out, v_meta_tokens, v_norm_w, v_w_in, v_sb_q_norm, v_sb_k_norm, v_gdn_conv_w, v_gdn_a_log, v_gdn_dt_bias, v_gdn_out_norm, v_hgrn_lb_logits, v_hgrn_out_norm, v_w_branch, v_w_out):
    depth = norm_w.shape[0]
    small_w = dict(norm_w=norm_w, sb_q_norm=sb_q_norm, sb_k_norm=sb_k_norm, gdn_a_log=gdn_a_log,
                   gdn_dt_bias=gdn_dt_bias, gdn_out_norm=gdn_out_norm, hgrn_lb_logits=hgrn_lb_logits,
                   hgrn_out_norm=hgrn_out_norm)
    small_m = dict(zip(_SMALL, (m_norm_w, m_sb_q_norm, m_sb_k_norm, m_gdn_a_log, m_gdn_dt_bias, m_gdn_out_norm,
                                m_hgrn_lb_logits, m_hgrn_out_norm)))
    small_v = dict(zip(_SMALL, (v_norm_w, v_sb_q_norm, v_sb_k_norm, v_gdn_a_log, v_gdn_dt_bias, v_gdn_out_norm,
                                v_hgrn_lb_logits, v_hgrn_out_norm)))

    w_in_g, w_br_g, w_out_g, conv_g, meta_g = gather_over_chips(
        [w_in.astype(BF16), w_branch.astype(BF16), w_out.astype(BF16), gdn_conv_w, meta_tokens])
    w_in_full = jnp.transpose(w_in_g, (1, 2, 0, 3)).reshape(depth, D_MODEL, N_IN)
    w_br_full = jnp.transpose(w_br_g, (1, 2, 3, 0, 4)).reshape(depth, N_BRANCHES, BRANCH, D_MODEL)
    w_out_full = jnp.transpose(w_out_g, (1, 0, 2, 3)).reshape(depth, D_MODEL, D_MODEL)
    conv_full = jnp.transpose(conv_g, (1, 2, 0, 3)).reshape(depth, 4, 3 * BRANCH)
    meta_full = jnp.transpose(meta_g, (1, 0, 2)).reshape(N_META, D_MODEL)

    sq_err, grad_x, d_meta, g = device_step(
        x[0], loss_target[0], meta_full, norm_w, w_in_full, sb_q_norm, sb_k_norm, conv_full, gdn_a_log, gdn_dt_bias,
        gdn_out_norm, hgrn_lb_logits, hgrn_out_norm, w_br_full, w_out_full)

    q4 = D_MODEL // 4
    n4 = N_IN // 4
    parts = [
        jnp.transpose(g["w_in"].reshape(depth, D_MODEL, 4, n4), (2, 0, 1, 3)).reshape(4, depth * D_MODEL, n4),
        jnp.transpose(g["w_branch"].reshape(depth, N_BRANCHES, BRANCH, 4, q4), (3, 0, 1, 2, 4)).reshape(4, -1, q4),
        jnp.transpose(g["w_out"].reshape(depth, 4, q4, D_MODEL), (1, 0, 2, 3)).reshape(4, depth * q4, D_MODEL),
        jnp.transpose(g["conv_w"].reshape(depth, 4, 4, 3 * BRANCH // 4), (2, 0, 1, 3)).reshape(4, depth * 4, -1),
        jnp.transpose(d_meta.reshape(N_META, 4, q4), (1, 0, 2)),
    ]
    received = scatter_over_chips(parts)
    core_sums = [sum_slabs(r) for r in received]
    sibling_sums = swap_with_sibling(core_sums)
    c = lax.axis_index("c")
    sharded = [(w_in, m_w_in, v_w_in), (w_branch, m_w_branch, v_w_branch), (w_out, m_w_out, v_w_out),
               (gdn_conv_w, m_gdn_conv_w, v_gdn_conv_w), (meta_tokens, m_meta_tokens, v_meta_tokens)]
    big = []
    for a, (w_, m_, v_) in enumerate(sharded):
        mine, other = core_sums[a], sibling_sums[a]
        first = jnp.where(c == 0, mine, other)
        second = jnp.where(c == 0, other, mine)
        shp = mine.shape
        outs = adamw(f"adamw_{a}", [first, second], w_.reshape(shp), m_.reshape(shp), v_.reshape(shp))
        big.append([o.reshape(w_.shape) for o in outs])

    small_shapes = [small_w[n].shape for n in _SMALL]
    gs = dict(norm_w=g["norm_w"], sb_q_norm=g["sb_qn"], sb_k_norm=g["sb_kn"], gdn_a_log=g["a_log"],
              gdn_dt_bias=g["dt_bias"], gdn_out_norm=g["gd_on"], hgrn_lb_logits=g["lb_logits"],
              hgrn_out_norm=g["hg_on"])
    n_lead = 128
    lead = [jnp.pad(sq_err.reshape(1), (0, n_lead - 1))]
    packed_g = gather_over_devices(_pack([gs[n] for n in _SMALL], lead))
    zeros_lead = [jnp.zeros((n_lead,), F32)]
    pw, pm, pv = (_pack([d[n] for n in _SMALL], zeros_lead) for d in (small_w, small_m, small_v))
    sg, sd, sm, sv_ = adamw("adamw_small", [packed_g[j] for j in range(8)], pw, pm, pv)
    loss = (0.5 / D_MODEL) * sg[0, 0]
    small = [dict(zip(_SMALL, _unpack(t, small_shapes, n_lead))) for t in (sg, sd, sm, sv_)]

    order = ("meta_tokens", "norm_w", "w_in", "sb_q_norm", "sb_k_norm", "gdn_conv_w", "gdn_a_log", "gdn_dt_bias",
             "gdn_out_norm", "hgrn_lb_logits", "hgrn_out_norm", "w_branch", "w_out")
    big_idx = dict(w_in=0, w_branch=1, w_out=2, gdn_conv_w=3, meta_tokens=4)
    result = [loss, grad_x[None]]
    for kind in range(4):
        for n in order:
            result.append(big[big_idx[n]][kind] if n in big_idx else small[kind][n])
    return tuple(result)
```

```python
import functools
import math

import jax
import jax.numpy as jnp
from jax import lax
from jax.experimental import pallas as pl
from jax.experimental.pallas import tpu as pltpu

F32, BF16 = jnp.float32, jnp.bfloat16
HIGHEST = lax.Precision.HIGHEST
MESH = pl.DeviceIdType.MESH

D_MODEL = 1024
BRANCH = 512
HEAD = 128
N_HEADS = 4
N_BRANCHES = 3
CHUNK = 64
N_META = 16
FRONT = 128
PAD_FRONT = FRONT - N_META
EPS = 1e-6
SB_KEY_BLOCK = 640
SB_LANE_GROUP = 128
HALO = 8
ROW_TILE = 320
SB_Q_TILE = 320
GDN_TILE = 320
LOSS_TILE = 128
VMEM_LIMIT = 56 * 1024 * 1024

ADAM_LR, ADAM_B1, ADAM_B2, ADAM_EPS, ADAM_WD, ADAM_STEP = 0.001, 0.9, 0.999, 1e-08, 0.01, 10

_C_SB_QKV = (0, 1536)
_C_SB_Z = (1536, 2048)
_C_GD_QKV = (2048, 3584)
_C_GD_Z = (3584, 4096)
_C_GD_BA = (4096, 4104)
_C_HG_QFI = (4104, 5640)
_C_HG_Z = (5640, 6152)
_C_MIX = (6152, 9224)
N_IN = 9224


def _mxu(a):
    return a.astype(BF16)


def _dg(a, b, ca, cb, hi=False):
    dn = (((ca,), (cb,)), ((), ()))
    if hi:
        return lax.dot_general(a, b, dn, precision=HIGHEST, preferred_element_type=F32)
    return lax.dot_general(_mxu(a), _mxu(b), dn, preferred_element_type=F32)


def _make_mm(hi):
    @jax.custom_vjp
    def nn(a, b):
        return _dg(a, b, 1, 0, hi)

    @jax.custom_vjp
    def nt(a, b):
        return _dg(a, b, 1, 1, hi)

    @jax.custom_vjp
    def tn(a, b):
        return _dg(a, b, 0, 0, hi)

    nn.defvjp(lambda a, b: (nn(a, b), (a, b)), lambda r, g: (nt(g, r[1]), tn(r[0], g)))
    nt.defvjp(lambda a, b: (nt(a, b), (a, b)), lambda r, g: (nn(g, r[1]), tn(g, r[0])))
    tn.defvjp(lambda a, b: (tn(a, b), (a, b)), lambda r, g: (nt(r[1], g), nn(r[0], g)))
    return nn, nt, tn


mm_nn, mm_nt, mm_tn = _make_mm(False)
mh_nn, mh_nt, mh_tn = _make_mm(True)


@jax.custom_vjp
def _mm_w(a, w, wz):
    return _dg(a, w, 1, 0)


def _mm_w_fwd(a, w, wz):
    return _dg(a, w, 1, 0), (a, w)


def _mm_w_bwd(res, g):
    a, w = res
    return _dg(g, w, 1, 1), jnp.zeros_like(w), _dg(a, g, 0, 0)


_mm_w.defvjp(_mm_w_fwd, _mm_w_bwd)


def mmw(a, w, wz):
    return _dg(a, w, 1, 0) if wz is None else _mm_w(a, w, wz)


@functools.partial(jax.custom_vjp, nondiff_argnums=(1,))
def roll_rows(x, s):
    return pltpu.roll(x, s, 0)


roll_rows.defvjp(lambda x, s: (pltpu.roll(x, s, 0), None),
                 lambda s, r, g: (pltpu.roll(g, g.shape[0] - s, 0),))


def _sigmoid(x):
    return 0.5 * (jnp.tanh(0.5 * x) + 1.0)


def _silu(x):
    return x * _sigmoid(x)


def _softplus(x):
    return jnp.maximum(x, 0.0) + jnp.log(1.0 + jnp.exp(-jnp.abs(x)))


def _rms(x, w):
    return x * lax.rsqrt(jnp.mean(x * x, axis=-1, keepdims=True) + EPS) * w


def _l2(x):
    return x * lax.rsqrt(jnp.sum(x * x, axis=-1, keepdims=True) + EPS)


def _heads(x, fn):
    return jnp.concatenate([fn(x[:, h * HEAD:(h + 1) * HEAD]) for h in range(N_HEADS)], axis=-1)


def _valid_rows(row0, n):
    rows = row0 + lax.broadcasted_iota(jnp.int32, (n, 1), 0)
    return (rows >= PAD_FRONT).astype(F32)


def _wz(wzs, i):
    return None if wzs is None else wzs[i]


def f_sb_pre(vals, p, w, wzs, row0):
    (h,), (norm_w, qn, kn) = vals, p
    raw = mmw(_rms(h, norm_w), w[0], _wz(wzs, 0))
    q = _heads(raw[:, :BRANCH], lambda t: _rms(t, qn))
    k = _heads(raw[:, BRANCH:2 * BRANCH], lambda t: _rms(t, kn))
    return [q, k, raw[:, 2 * BRANCH:]]


def f_gd_pre(vals, p, w, wzs, row0):
    (h_ext,), (norm_w, conv_w, a_log, dt_bias) = vals, p
    tm = h_ext.shape[0] - HALO
    raw = mmw(_rms(h_ext, norm_w), w[0], _wz(wzs, 0))
    x = raw[:, :3 * BRANCH]
    y = conv_w[3:4] * x
    for i in range(3):
        y = y + conv_w[i:i + 1] * roll_rows(x, 3 - i)
    y = _silu(y[HALO:])
    gq = _heads(y[:, :BRANCH], _l2) * (HEAD ** -0.5)
    gk = _heads(y[:, BRANCH:2 * BRANCH], _l2)
    gv = y[:, 2 * BRANCH:]
    slab = raw[HALO:, 3 * BRANCH:]
    lane = lax.broadcasted_iota(jnp.int32, (1, HEAD), 1)
    beta = _sigmoid(slab) * _valid_rows(row0, tm)
    g = -jnp.exp(a_log) * _softplus(slab + dt_bias)
    bg = jnp.where(lane < N_HEADS, beta, jnp.where(lane < 2 * N_HEADS, g, 0.0))
    return [gq, gk, gv, bg]


def f_hg_pre(vals, p, w, wzs, row0):
    (h,), (norm_w, lb) = vals, p
    raw = mmw(_rms(h, norm_w), w[0], _wz(wzs, 0))
    hq = _silu(raw[:, :BRANCH])
    fp = raw[:, BRANCH:2 * BRANCH]
    forget = lb + (1.0 - lb) * _sigmoid(fp)
    hk = (1.0 - lb) * _sigmoid(-fp)
    hv = raw[:, 2 * BRANCH:] * _valid_rows(row0, h.shape[0])
    return [hq, hk, hv, jnp.log(forget)]


def _merge_u(h, o, norm_w, out_norm, w, wzs, normed):
    xn = _rms(h, norm_w)
    z = mmw(xn, w[0], _wz(wzs, 0))
    mix = mmw(xn, w[1], _wz(wzs, 1))
    if normed:
        o = _heads(o, lambda t: _rms(t, out_norm))
    return _sigmoid(mix) * mmw(o * _silu(z), w[2], _wz(wzs, 2))


def make_f_merge(normed, with_prev):
    def f(vals, p, w, wzs, row0):
        u = _merge_u(vals[0], vals[1], p[0], p[1], w, wzs, normed)
        return [vals[2] + u] if with_prev else [u]
    return f


def f_out(vals, p, w, wzs, row0):
    h, y = vals
    return [h + mmw(y, w[0], _wz(wzs, 0))]


def _full_spec(a):
    nd = a.ndim
    return pl.BlockSpec(a.shape, lambda i, _nd=nd: (0,) * _nd)


def stage_fwd(name, fn, rows, params, weights, out_widths, *, halo=False, tm=ROW_TILE):
    t_len = rows[0].shape[0]
    n = t_len // tm
    nr, npar, nw = len(rows), len(params), len(weights)

    def body(*refs):
        i = pl.program_id(0)
        refs = list(refs)
        prev_ref = refs.pop(0) if halo else None
        row_refs, refs = refs[:nr], refs[nr:]
        par_refs, refs = refs[:npar], refs[npar:]
        w_refs, out_refs = refs[:nw], refs[nw:]
        vals = [r[...] for r in row_refs]
        if halo:
            prev = jnp.where(i == 0, 0.0, prev_ref[...])
            vals[0] = jnp.concatenate([prev, vals[0]], axis=0)
        outs = fn(vals, [r[...] for r in par_refs], [r[...] for r in w_refs], None, i * tm)
        for o_ref, o in zip(out_refs, outs):
            o_ref[...] = o

    in_specs, args = [], []
    if halo:
        in_specs.append(pl.BlockSpec((HALO, rows[0].shape[1]),
                                     lambda i: (jnp.maximum(i * (tm // HALO) - 1, 0), 0)))
        args.append(rows[0])
    for r in rows:
        in_specs.append(pl.BlockSpec((tm, r.shape[1]), lambda i: (i, 0)))
        args.append(r)
    for a in list(params) + list(weights):
        in_specs.append(_full_spec(a))
        args.append(a)
    return pl.pallas_call(
        body, grid=(n,), in_specs=in_specs,
        out_specs=[pl.BlockSpec((tm, wd), lambda i: (i, 0)) for wd in out_widths],
        out_shape=[jax.ShapeDtypeStruct((t_len, wd), F32) for wd in out_widths],
        compiler_params=pltpu.CompilerParams(dimension_semantics=("arbitrary",), vmem_limit_bytes=VMEM_LIMIT),
        name=name,
    )(*args)


def stage_vjp(name, fn, rows, params, weights, cts, diff_rows, *, halo=False, acc_in=None, tm=ROW_TILE):
    t_len = rows[0].shape[0]
    n = t_len // tm
    nr, npar, nw, nct, nd = len(rows), len(params), len(weights), len(cts), len(diff_rows)
    has_acc = acc_in is not None

    def body(*refs):
        i = pl.program_id(0)
        tile = n - 1 - i
        refs = list(refs)
        prev_ref = refs.pop(0) if halo else None
        row_refs, refs = refs[:nr], refs[nr:]
        par_refs, refs = refs[:npar], refs[npar:]
        w_refs, refs = refs[:nw], refs[nw:]
        ct_refs, refs = refs[:nct], refs[nct:]
        acc_ref = refs.pop(0) if has_acc else None
        drow_refs, refs = refs[:nd], refs[nd:]
        dpar_refs, refs = refs[:npar], refs[npar:]
        dw_refs, refs = refs[:nw], refs[nw:]
        carry_ref = refs[0] if halo else None

        vals = [r[...] for r in row_refs]
        if halo:
            prev = jnp.where(tile == 0, 0.0, prev_ref[...])
            vals[0] = jnp.concatenate([prev, vals[0]], axis=0)
        pvals = [r[...] for r in par_refs]
        wvals = [r[...] for r in w_refs]
        wzs = [jnp.zeros(w.shape, F32) for w in wvals]

        def f(dvals, pv, wz):
            full = list(vals)
            for k, idx in enumerate(diff_rows):
                full[idx] = dvals[k]
            return fn(full, pv, wvals, wz, tile * tm)

        _, vjp = jax.vjp(f, [vals[k] for k in diff_rows], pvals, wzs)
        d_rows, d_par, d_w = vjp([r[...] for r in ct_refs])

        @pl.when(i == 0)
        def _():
            for r in list(dpar_refs) + list(dw_refs):
                r[...] = jnp.zeros(r.shape, F32)

        for r, g in zip(list(dpar_refs) + list(dw_refs), list(d_par) + list(d_w)):
            r[...] += g
        for k, (r, g) in enumerate(zip(drow_refs, d_rows)):
            if k == 0 and diff_rows[0] == 0:
                if halo:
                    g_ext = g
                    g = g_ext[HALO:]
                    tail = g[tm - HALO:] + jnp.where(i == 0, 0.0, carry_ref[...])
                    g = jnp.concatenate([g[:tm - HALO], tail], axis=0)
                    carry_ref[...] = g_ext[:HALO]
                if has_acc:
                    g = g + acc_ref[...]
            r[...] = g

    rev = lambda i: (n - 1 - i, 0)
    in_specs, args = [], []
    if halo:
        in_specs.append(pl.BlockSpec((HALO, rows[0].shape[1]),
                                     lambda i: (jnp.maximum((n - 1 - i) * (tm // HALO) - 1, 0), 0)))
        args.append(rows[0])
    for r in list(rows):
        in_specs.append(pl.BlockSpec((tm, r.shape[1]), rev))
        args.append(r)
    for a in list(params) + list(weights):
        in_specs.append(_full_spec(a))
        args.append(a)
    for c in cts:
        in_specs.append(pl.BlockSpec((tm, c.shape[1]), rev))
        args.append(c)
    if has_acc:
        in_specs.append(pl.BlockSpec((tm, acc_in.shape[1]), rev))
        args.append(acc_in)
    out_specs = [pl.BlockSpec((tm, rows[k].shape[1]), rev) for k in diff_rows]
    out_shape = [jax.ShapeDtypeStruct(rows[k].shape, F32) for k in diff_rows]
    for a in list(params) + list(weights):
        out_specs.append(_full_spec(a))
        out_shape.append(jax.ShapeDtypeStruct(a.shape, F32))
    scratch = [pltpu.VMEM((HALO, rows[0].shape[1]), F32)] if halo else []
    outs = pl.pallas_call(
        body, grid=(n,), in_specs=in_specs, out_specs=out_specs, out_shape=out_shape, scratch_shapes=scratch,
        compiler_params=pltpu.CompilerParams(dimension_semantics=("arbitrary",), vmem_limit_bytes=VMEM_LIMIT),
        name=name,
    )(*args)
    return outs[:nd], outs[nd:nd + npar], outs[nd + npar:]


def _sb_tile(qb, kj, t_idx, ks):
    z = _dg(qb, kj, 1, 1) * (HEAD ** -0.5)
    s_idx = ks + lax.broadcasted_iota(jnp.int32, (1, SB_KEY_BLOCK), 1)
    mask = (s_idx < t_idx) & (s_idx >= PAD_FRONT)
    sp = jnp.log(1.0 + jnp.exp(-jnp.abs(z)))
    ls = jnp.minimum(z, 0.0) - sp
    lk = jnp.where(mask, ls - z, 0.0)
    return mask, ls, lk


def _tri(n, kind):
    r = lax.broadcasted_iota(jnp.int32, (n, n), 0)
    c = lax.broadcasted_iota(jnp.int32, (n, n), 1)
    return {"gt": r > c, "ge": r >= c, "eq": r == c}[kind]


def _suffix_sums(x, tri, carry, inclusive):
    groups = []
    for g in reversed(range(x.shape[1] // SB_LANE_GROUP)):
        xg = x[:, g * SB_LANE_GROUP:(g + 1) * SB_LANE_GROUP]
        sg = _dg(xg, tri, 1, 0, hi=True)
        groups.append(sg + carry)
        carry = carry + (sg[:, 0:1] if inclusive else sg[:, 0:1] + xg[:, 0:1])
    return jnp.concatenate(groups[::-1], axis=1), carry


def sb_fwd(q, k, v, *, tq=SB_Q_TILE):
    t_len = q.shape[0]
    nq = t_len // tq
    kb = SB_KEY_BLOCK

    def body(q_ref, k_ref, v_ref, o_ref):
        i = pl.program_id(1)
        qb = _mxu(q_ref[...])
        t_idx = i * tq + lax.broadcasted_iota(jnp.int32, (tq, 1), 0)
        after = _tri(SB_LANE_GROUP, "gt").astype(F32)
        o_ref[...] = jnp.zeros(o_ref.shape, F32)
        n_blocks = ((i + 1) * tq + kb - 1) // kb

        def step(jj, run):
            ks = pl.multiple_of((n_blocks - 1 - jj) * kb, kb)
            kj = _mxu(k_ref[pl.ds(ks, kb), :])
            vj = _mxu(v_ref[pl.ds(ks, kb), :])
            mask, ls, lk = _sb_tile(qb, kj, t_idx, ks)
            passed, run = _suffix_sums(lk, after, run, False)
            a = jnp.where(mask, jnp.exp(ls + passed), 0.0)
            o_ref[...] += _dg(a, vj, 1, 0)
            return run

        lax.fori_loop(0, n_blocks, step, jnp.zeros((tq, 1), F32))

    col = pl.BlockSpec((t_len, HEAD), lambda h, i: (0, h))
    tile = pl.BlockSpec((tq, HEAD), lambda h, i: (i, h))
    return pl.pallas_call(
        body, grid=(N_HEADS, nq), in_specs=[tile, col, col], out_specs=tile,
        out_shape=jax.ShapeDtypeStruct((t_len, BRANCH), F32),
        compiler_params=pltpu.CompilerParams(dimension_semantics=("arbitrary", "arbitrary"),
                                             vmem_limit_bytes=VMEM_LIMIT),
        name="sb_fwd",
    )(q, k, v)


def sb_bwd(q, k, v, o, do, *, tq=SB_Q_TILE):
    t_len = q.shape[0]
    nq = t_len // tq
    kb = SB_KEY_BLOCK
    scale = HEAD ** -0.5

    def body(q_ref, k_ref, v_ref, o_ref, do_ref, dq_ref, dk_ref, dv_ref):
        i = pl.program_id(1)

        @pl.when(i == 0)
        def _():
            dk_ref[...] = jnp.zeros(dk_ref.shape, F32)
            dv_ref[...] = jnp.zeros(dv_ref.shape, F32)

        qb = _mxu(q_ref[...])
        do_f = do_ref[...]
        dob = _mxu(do_f)
        total = jnp.sum(dob.astype(F32) * o_ref[...], axis=-1, keepdims=True)
        t_idx = i * tq + lax.broadcasted_iota(jnp.int32, (tq, 1), 0)
        after = _tri(SB_LANE_GROUP, "gt").astype(F32)
        from_here = _tri(SB_LANE_GROUP, "ge").astype(F32)
        dq_ref[...] = jnp.zeros(dq_ref.shape, F32)
        n_blocks = ((i + 1) * tq + kb - 1) // kb

        def step(jj, carry):
            run, run_dl = carry
            ks = pl.multiple_of((n_blocks - 1 - jj) * kb, kb)
            kj = _mxu(k_ref[pl.ds(ks, kb), :])
            vj = _mxu(v_ref[pl.ds(ks, kb), :])
            mask, ls, lk = _sb_tile(qb, kj, t_idx, ks)
            passed, run = _suffix_sums(lk, after, run, False)
            a = jnp.where(mask, jnp.exp(ls + passed), 0.0)
            ab = _mxu(a)
            dl = ab.astype(F32) * _dg(dob, vj, 1, 1)
            from_s, run_dl = _suffix_sums(dl, from_here, run_dl, True)
            before = total - from_s
            sig = jnp.exp(ls)
            dz = (dl * (1.0 - sig) - jnp.where(mask, sig * before, 0.0)) * scale
            dzb = _mxu(dz)
            dq_ref[...] += _dg(dzb, kj, 1, 0)
            dk_ref[pl.ds(ks, kb), :] += _dg(dzb, qb, 0, 0)
            dv_ref[pl.ds(ks, kb), :] += _dg(ab, dob, 0, 0)
            return run, run_dl

        zero = jnp.zeros((tq, 1), F32)
        lax.fori_loop(0, n_blocks, step, (zero, zero))

    col = pl.BlockSpec((t_len, HEAD), lambda h, i: (0, h))
    tile = pl.BlockSpec((tq, HEAD), lambda h, i: (i, h))
    full = jax.ShapeDtypeStruct((t_len, BRANCH), F32)
    return pl.pallas_call(
        body, grid=(N_HEADS, nq), in_specs=[tile, col, col, tile, tile], out_specs=[tile, col, col],
        out_shape=[full, full, full],
        compiler_params=pltpu.CompilerParams(dimension_semantics=("arbitrary", "arbitrary"),
                                             vmem_limit_bytes=VMEM_LIMIT),
        name="sb_bwd",
    )(q, k, v, o, do)


def _bdg(a, b, ca, cb, hi=False):
    dn = (((ca,), (cb,)), ((0,), (0,)))
    if hi:
        return lax.dot_general(a, b, dn, precision=HIGHEST, preferred_element_type=F32)
    return lax.dot_general(_mxu(a), _mxu(b), dn, preferred_element_type=F32)


def _make_bmm(hi):
    @jax.custom_vjp
    def nn(a, b):
        return _bdg(a, b, 2, 1, hi)

    @jax.custom_vjp
    def nt(a, b):
        return _bdg(a, b, 2, 2, hi)

    @jax.custom_vjp
    def tn(a, b):
        return _bdg(a, b, 1, 1, hi)

    nn.defvjp(lambda a, b: (nn(a, b), (a, b)), lambda r, g: (nt(g, r[1]), tn(r[0], g)))
    nt.defvjp(lambda a, b: (nt(a, b), (a, b)), lambda r, g: (nn(g, r[1]), tn(g, r[0])))
    tn.defvjp(lambda a, b: (tn(a, b), (a, b)), lambda r, g: (nt(r[1], g), nn(r[0], g)))
    return nn, nt, tn


bm_nn, bm_nt, bm_tn = _make_bmm(False)
bh_nn, bh_nt, bh_tn = _make_bmm(True)


def gdn_tile(state, q, k, v, bg):
    c = CHUNK
    nc = q.shape[0] // c
    pairs = [(ci, h) for ci in range(nc) for h in range(N_HEADS)]
    nb = len(pairs)
    rows = lambda ci: slice(ci * c, (ci + 1) * c)
    split = lambda x: jnp.stack([x[rows(ci), h * HEAD:(h + 1) * HEAD] for ci, h in pairs])
    qs, ks, vs = split(q), split(k), split(v)
    beta = jnp.stack([bg[rows(ci), h:h + 1] for ci, h in pairs])
    g = jnp.stack([bg[rows(ci), N_HEADS + h:N_HEADS + h + 1] for ci, h in pairs])
    causal = jnp.broadcast_to(_tri(c, "ge")[None], (nb, c, c))
    strict = jnp.broadcast_to(_tri(c, "gt")[None], (nb, c, c))
    eye = jnp.broadcast_to(_tri(c, "eq")[None], (nb, c, c)).astype(F32)
    g_lanes = bh_nn(causal.astype(F32), g * jnp.ones((1, 1, HEAD), F32))
    g_col = g_lanes[:, :, :c]
    g_row = bh_nn(jnp.ones((nb, c, c), F32), eye * g_col)
    decay = jnp.where(causal, jnp.exp(jnp.where(causal, g_col - g_row, 0.0)), 0.0)
    kb = ks * beta
    m = jnp.where(strict, bm_nt(kb, ks) * decay, 0.0)
    inv = eye - m
    p = bh_nn(m, m)
    for level in range(5):
        inv = inv + bh_nn(inv, p)
        if level < 4:
            p = bh_nn(p, p)
    g1 = g_lanes[:, :, 0:1]
    g_last = g1[:, c - 1:c, :]
    u = bh_nn(inv, vs * beta)
    w = bh_nn(inv, kb * jnp.exp(g1))
    a_qk = jnp.where(causal, bm_nt(qs, ks) * decay, 0.0)
    qd = qs * jnp.exp(g1)
    kd = ks * jnp.exp(g_last - g1)
    gl = jnp.exp(g_last)
    outs = []
    for ci in range(nc):
        sl = slice(ci * N_HEADS, (ci + 1) * N_HEADS)
        v_new = u[sl] - bm_nn(w[sl], state)
        o = bm_nn(qd[sl], state) + bm_nn(a_qk[sl], v_new)
        state = state * gl[sl] + bm_tn(kd[sl], v_new)
        outs.append(jnp.concatenate([o[h] for h in range(N_HEADS)], axis=1))
    return state, jnp.concatenate(outs, axis=0)


def gdn_fwd(q, k, v, bg, *, tg=GDN_TILE):
    t_len = q.shape[0]
    n = t_len // tg

    def body(q_ref, k_ref, v_ref, bg_ref, o_ref, s_all_ref, s_ref):
        @pl.when(pl.program_id(0) == 0)
        def _():
            s_ref[...] = jnp.zeros(s_ref.shape, F32)

        s_in = s_ref[...]
        s_all_ref[0] = s_in
        s_out, o = gdn_tile(s_in, q_ref[...], k_ref[...], v_ref[...], bg_ref[...])
        o_ref[...] = o
        s_ref[...] = s_out

    row = lambda wd: pl.BlockSpec((tg, wd), lambda i: (i, 0))
    return pl.pallas_call(
        body, grid=(n,), in_specs=[row(BRANCH), row(BRANCH), row(BRANCH), row(HEAD)],
        out_specs=[row(BRANCH), pl.BlockSpec((1, N_HEADS, HEAD, HEAD), lambda i: (i, 0, 0, 0))],
        out_shape=[jax.ShapeDtypeStruct((t_len, BRANCH), F32),
                   jax.ShapeDtypeStruct((n, N_HEADS, HEAD, HEAD), F32)],
        scratch_shapes=[pltpu.VMEM((N_HEADS, HEAD, HEAD), F32)],
        compiler_params=pltpu.CompilerParams(dimension_semantics=("arbitrary",), vmem_limit_bytes=VMEM_LIMIT),
        name="gdn_fwd",
    )(q, k, v, bg)


def gdn_bwd(q, k, v, bg, s_all, do, *, tg=GDN_TILE):
    t_len = q.shape[0]
    n = t_len // tg

    def body(q_ref, k_ref, v_ref, bg_ref, s_all_ref, do_ref, dq_ref, dk_ref, dv_ref, dbg_ref, ds_ref):
        @pl.when(pl.program_id(0) == 0)
        def _():
            ds_ref[...] = jnp.zeros(ds_ref.shape, F32)

        _, vjp = jax.vjp(gdn_tile, s_all_ref[0], q_ref[...], k_ref[...], v_ref[...], bg_ref[...])
        ds, dq, dk, dv, dbg = vjp((ds_ref[...], do_ref[...]))
        ds_ref[...] = ds
        dq_ref[...] = dq
        dk_ref[...] = dk
        dv_ref[...] = dv
        dbg_ref[...] = dbg

    row = lambda wd: pl.BlockSpec((tg, wd), lambda i: (n - 1 - i, 0))
    wide, slab = jax.ShapeDtypeStruct((t_len, BRANCH), F32), jax.ShapeDtypeStruct((t_len, HEAD), F32)
    return pl.pallas_call(
        body, grid=(n,),
        in_specs=[row(BRANCH), row(BRANCH), row(BRANCH), row(HEAD),
                  pl.BlockSpec((1, N_HEADS, HEAD, HEAD), lambda i: (n - 1 - i, 0, 0, 0)), row(BRANCH)],
        out_specs=[row(BRANCH), row(BRANCH), row(BRANCH), row(HEAD)],
        out_shape=[wide, wide, wide, slab],
        scratch_shapes=[pltpu.VMEM((N_HEADS, HEAD, HEAD), F32)],
        compiler_params=pltpu.CompilerParams(dimension_semantics=("arbitrary",), vmem_limit_bytes=VMEM_LIMIT),
        name="gdn_bwd",
    )(q, k, v, bg, s_all, do)


def hgrn_chunk(state_t, q, k, v, g):
    c = CHUNK
    gc = mh_nn(_tri(c, "ge").astype(F32), g)
    o = mm_nt(q * jnp.exp(gc), state_t)
    row = lax.broadcasted_iota(jnp.int32, (c, 1), 0)
    for d in range(c):
        kr, gr, vr = (k, gc, v) if d == 0 else (roll_rows(k, d), roll_rows(gc, d), roll_rows(v, d))
        ok = row >= d
        e = jnp.exp(jnp.where(ok, gc - gr, 0.0))
        a = jnp.where(ok, jnp.sum(q * kr * e, axis=-1, keepdims=True), 0.0)
        o = o + a * vr
    g_end = gc[c - 1:c, :]
    state_t = state_t * jnp.exp(g_end) + mm_tn(v, k * jnp.exp(g_end - gc))
    return state_t, o


def hgrn_fwd(q, k, v, g, *, tg=ROW_TILE):
    t_len = q.shape[0]
    n, nc = t_len // tg, tg // CHUNK

    def body(q_ref, k_ref, v_ref, g_ref, o_ref, s_all_ref, s_ref):
        @pl.when(pl.program_id(1) == 0)
        def _():
            s_ref[...] = jnp.zeros(s_ref.shape, F32)

        def step(c, _):
            r = pl.ds(pl.multiple_of(c * CHUNK, CHUNK), CHUNK)
            s_in = s_ref[...]
            s_all_ref[c, 0] = s_in
            s_out, o = hgrn_chunk(s_in, q_ref[r, :], k_ref[r, :], v_ref[r, :], g_ref[r, :])
            o_ref[r, :] = o
            s_ref[...] = s_out
            return 0

        lax.fori_loop(0, nc, step, 0)

    row = pl.BlockSpec((tg, HEAD), lambda h, i: (i, h))
    return pl.pallas_call(
        body, grid=(N_HEADS, n), in_specs=[row, row, row, row],
        out_specs=[row, pl.BlockSpec((nc, 1, HEAD, HEAD), lambda h, i: (i, h, 0, 0))],
        out_shape=[jax.ShapeDtypeStruct((t_len, BRANCH), F32),
                   jax.ShapeDtypeStruct((t_len // CHUNK, N_HEADS, HEAD, HEAD), F32)],
        scratch_shapes=[pltpu.VMEM((HEAD, HEAD), F32)],
        compiler_params=pltpu.CompilerParams(dimension_semantics=("arbitrary", "arbitrary"),
                                             vmem_limit_bytes=VMEM_LIMIT),
        name="hgrn_fwd",
    )(q, k, v, g)


def hgrn_bwd(q, k, v, g, s_all, do, *, tg=ROW_TILE):
    t_len = q.shape[0]
    n, nc = t_len // tg, tg // CHUNK

    def body(q_ref, k_ref, v_ref, g_ref, s_all_ref, do_ref, dq_ref, dk_ref, dv_ref, dg_ref, ds_ref):
        @pl.when(pl.program_id(1) == 0)
        def _():
            ds_ref[...] = jnp.zeros(ds_ref.shape, F32)

        def step(cc, _):
            c = nc - 1 - cc
            r = pl.ds(pl.multiple_of(c * CHUNK, CHUNK), CHUNK)
            _, vjp = jax.vjp(hgrn_chunk, s_all_ref[c, 0], q_ref[r, :], k_ref[r, :], v_ref[r, :], g_ref[r, :])
            ds, dq, dk, dv, dg = vjp((ds_ref[...], do_ref[r, :]))
            ds_ref[...] = ds
            dq_ref[r, :] = dq
            dk_ref[r, :] = dk
            dv_ref[r, :] = dv
            dg_ref[r, :] = dg
            return 0

        lax.fori_loop(0, nc, step, 0)

    row = pl.BlockSpec((tg, HEAD), lambda h, i: (n - 1 - i, h))
    wide = jax.ShapeDtypeStruct((t_len, BRANCH), F32)
    return pl.pallas_call(
        body, grid=(N_HEADS, n),
        in_specs=[row, row, row, row, pl.BlockSpec((nc, 1, HEAD, HEAD), lambda h, i: (n - 1 - i, h, 0, 0)), row],
        out_specs=[row, row, row, row], out_shape=[wide, wide, wide, wide],
        scratch_shapes=[pltpu.VMEM((HEAD, HEAD), F32)],
        compiler_params=pltpu.CompilerParams(dimension_semantics=("arbitrary", "arbitrary"),
                                             vmem_limit_bytes=VMEM_LIMIT),
        name="hgrn_bwd",
    )(q, k, v, g, s_all, do)


def loss_head(h, target, *, tl=LOSS_TILE):
    seq = target.shape[0]
    n = seq // tl
    off = FRONT // tl

    def body(h_ref, t_ref, sq_ref, dy_ref):
        @pl.when(pl.program_id(0) == 0)
        def _():
            sq_ref[...] = jnp.zeros(sq_ref.shape, F32)

        err = h_ref[...] - t_ref[...]
        sq_ref[...] += jnp.sum(err * err, keepdims=True)
        dy_ref[...] = err * (1.0 / D_MODEL)

    return pl.pallas_call(
        body, grid=(n,),
        in_specs=[pl.BlockSpec((tl, D_MODEL), lambda i: (i + off, 0)), pl.BlockSpec((tl, D_MODEL), lambda i: (i, 0))],
        out_specs=[pl.BlockSpec((1, 1), lambda i: (0, 0)), pl.BlockSpec((tl, D_MODEL), lambda i: (i, 0))],
        out_shape=[jax.ShapeDtypeStruct((1, 1), F32), jax.ShapeDtypeStruct((seq, D_MODEL), F32)],
        compiler_params=pltpu.CompilerParams(dimension_semantics=("arbitrary",)),
        name="loss_head",
    )(h, target)


def _pad_lanes(a, lo, n=HEAD):
    return jnp.pad(a.astype(F32), (lo, n - lo - a.shape[0])).reshape(1, n)


def _lower_bounds(logits):
    p = jax.nn.softmax(logits.astype(F32), axis=0)
    return jnp.cumsum(p, axis=0) - p[0:1]


def _layer_weights(w_in_l, w_branch_l, w_out_l):
    c = lambda r: w_in_l[:, r[0]:r[1]]
    ba = jnp.pad(c(_C_GD_BA), ((0, 0), (0, HEAD - 2 * N_HEADS)))
    mix = c(_C_MIX)
    return dict(
        sb=c(_C_SB_QKV), gd=jnp.concatenate([c(_C_GD_QKV), ba], axis=1), hg=c(_C_HG_QFI),
        z=[c(_C_SB_Z), c(_C_GD_Z), c(_C_HG_Z)],
        mix=[mix[:, b * D_MODEL:(b + 1) * D_MODEL] for b in range(N_BRANCHES)],
        br=[w_branch_l[b] for b in range(N_BRANCHES)], out=w_out_l)


def device_step(x, target, meta, norm_w, w_in, sb_qn, sb_kn, conv_w, a_log, dt_bias, gd_on, lb_logits, hg_on,
                w_branch, w_out):
    depth = norm_w.shape[0]
    lbs, lb_vjp = jax.vjp(_lower_bounds, lb_logits)
    h = jnp.concatenate([jnp.zeros((PAD_FRONT, D_MODEL), F32), meta, x], axis=0)
    row = lambda a: a.reshape(1, -1)
    saved = []
    for l in range(depth):
        w = _layer_weights(w_in[l], w_branch[l], w_out[l])
        nw = row(norm_w[l])
        p_sb = [nw, row(sb_qn[l]), row(sb_kn[l])]
        p_gd = [nw, conv_w[l], _pad_lanes(a_log[l], N_HEADS), _pad_lanes(dt_bias[l], N_HEADS)]
        p_hg = [nw, row(lbs[l])]
        out_norms = [row(sb_qn[l]), row(gd_on[l]), row(hg_on[l])]
        sq, sk, sv = stage_fwd("sb_pre", f_sb_pre, [h], p_sb, [w["sb"]], [BRANCH] * 3)
        o_sb = sb_fwd(sq, sk, sv)
        gq, gk, gv, bg = stage_fwd("gd_pre", f_gd_pre, [h], p_gd, [w["gd"]], [BRANCH] * 3 + [HEAD], halo=True)
        o_gd, gd_states = gdn_fwd(gq, gk, gv, bg)
        hq, hk, hv, hg = stage_fwd("hg_pre", f_hg_pre, [h], p_hg, [w["hg"]], [BRANCH] * 4)
        o_hg, hg_states = hgrn_fwd(hq, hk, hv, hg)
        branch_o = [o_sb, o_gd, o_hg]
        y = None
        for b in range(N_BRANCHES):
            rows = [h, branch_o[b]] + ([] if y is None else [y])
            (y,) = stage_fwd(f"merge{b}", make_f_merge(b > 0, y is not None), rows, [nw, out_norms[b]],
                             [w["z"][b], w["mix"][b], w["br"][b]], [D_MODEL])
        (h_next,) = stage_fwd("out_proj", f_out, [h, y], [], [w["out"]], [D_MODEL])
        saved.append(dict(h=h, w=w, p_sb=p_sb, p_gd=p_gd, p_hg=p_hg, out_norms=out_norms, y=y,
                          sb=(sq, sk, sv, o_sb), gd=(gq, gk, gv, bg, gd_states, o_gd),
                          hg=(hq, hk, hv, hg, hg_states, o_hg)))
        h = h_next

    sq_err, dy = loss_head(h, target)
    dh = jnp.concatenate([jnp.zeros((FRONT, D_MODEL), F32), dy], axis=0)

    g = dict(norm_w=[], w_in=[], sb_qn=[], sb_kn=[], conv_w=[], a_log=[], dt_bias=[], gd_on=[], lb=[], hg_on=[],
             w_branch=[], w_out=[])
    for l in reversed(range(depth)):
        s = saved[l]
        w, hl = s["w"], s["h"]
        nw = s["p_sb"][0]
        (dy_,), _, (d_wout,) = stage_vjp("out_proj_b", f_out, [hl, s["y"]], [], [w["out"]], [dh], [1])
        d_norm = jnp.zeros((1, D_MODEL), F32)
        d_o, d_on, d_wz, d_wmix, d_wbr = [None] * 3, [None] * 3, [None] * 3, [None] * 3, [None] * 3
        branch_o = [s["sb"][3], s["gd"][5], s["hg"][5]]
        for b in reversed(range(N_BRANCHES)):
            (dh, d_o[b]), (dn, d_on[b]), (d_wz[b], d_wmix[b], d_wbr[b]) = stage_vjp(
                f"merge{b}_b", make_f_merge(b > 0, False), [hl, branch_o[b]], [nw, s["out_norms"][b]],
                [w["z"][b], w["mix"][b], w["br"][b]], [dy_], [0, 1], acc_in=dh)
            d_norm = d_norm + dn
        hq, hk, hv, hg, hg_states, _ = s["hg"]
        d_hg = hgrn_bwd(hq, hk, hv, hg, hg_states, d_o[2])
        (dh,), (dn, d_lb), (d_whg,) = stage_vjp("hg_pre_b", f_hg_pre, [hl], s["p_hg"], [w["hg"]], list(d_hg), [0],
                                                acc_in=dh)
        d_norm = d_norm + dn
        gq, gk, gv, bg, gd_states, _ = s["gd"]
        d_gd = gdn_bwd(gq, gk, gv, bg, gd_states, d_o[1])
        (dh,), (dn, d_conv, d_alog, d_dtb), (d_wgd,) = stage_vjp(
            "gd_pre_b", f_gd_pre, [hl], s["p_gd"], [w["gd"]], list(d_gd), [0], halo=True, acc_in=dh)
        d_norm = d_norm + dn
        sq, sk, sv, o_sb = s["sb"]
        d_sb = sb_bwd(sq, sk, sv, o_sb, d_o[0])
        (dh,), (dn, d_qn, d_kn), (d_wsb,) = stage_vjp("sb_pre_b", f_sb_pre, [hl], s["p_sb"], [w["sb"]], list(d_sb),
                                                      [0], acc_in=dh)
        d_norm = d_norm + dn
        g["norm_w"].append(d_norm[0])
        g["w_in"].append(jnp.concatenate(
            [d_wsb, d_wz[0], d_wgd[:, :3 * BRANCH], d_wz[1], d_wgd[:, 3 * BRANCH:3 * BRANCH + 2 * N_HEADS],
             d_whg, d_wz[2]] + d_wmix, axis=1))
        g["sb_qn"].append(d_qn[0])
        g["sb_kn"].append(d_kn[0])
        g["conv_w"].append(d_conv)
        g["a_log"].append(d_alog[0, N_HEADS:2 * N_HEADS])
        g["dt_bias"].append(d_dtb[0, N_HEADS:2 * N_HEADS])
        g["gd_on"].append(d_on[1][0])
        g["hg_on"].append(d_on[2][0])
        g["lb"].append(d_lb[0])
        g["w_branch"].append(jnp.stack(d_wbr))
        g["w_out"].append(d_wout)
    g = {k: jnp.stack(v[::-1]) for k, v in g.items()}
    (g["lb_logits"],) = lb_vjp(g.pop("lb"))
    return sq_err, dh[FRONT:], dh[PAD_FRONT:FRONT], g


ANY = pl.BlockSpec(memory_space=pl.ANY)


def _coords():
    return lax.axis_index("x"), lax.axis_index("y"), lax.axis_index("c")


def _other_chips(x, y):
    return [(2 * px + py, (px, py)) for px, py in ((1 - x, y), (x, 1 - y), (1 - x, 1 - y))]


def gather_over_chips(shards):
    na = len(shards)

    def body(*refs):
        ins, outs = refs[:na], refs[na:2 * na]
        send_sems, recv_sems, local_sems = refs[2 * na:]
        x, y, c = _coords()
        me = 2 * x + y
        copies = []
        for a in range(na):
            own = pltpu.make_async_copy(ins[a], outs[a].at[me], local_sems.at[a])
            own.start()
            copies.append(own)
        sends = []
        for a in range(na):
            for k, (_, (px, py)) in enumerate(_other_chips(x, y)):
                cp = pltpu.make_async_remote_copy(
                    src_ref=ins[a], dst_ref=outs[a].at[me], send_sem=send_sems.at[a, k], recv_sem=recv_sems.at[a, k],
                    device_id=(px, py, c), device_id_type=MESH)
                cp.start()
                sends.append(cp)
        for a in range(na):
            for k, (p, (px, py)) in enumerate(_other_chips(x, y)):
                pltpu.make_async_remote_copy(
                    src_ref=ins[a], dst_ref=outs[a].at[p], send_sem=send_sems.at[a, k], recv_sem=recv_sems.at[a, k],
                    device_id=(px, py, c), device_id_type=MESH).wait_recv()
        for cp in sends:
            cp.wait_send()
        for cp in copies:
            cp.wait()

    return pl.pallas_call(
        body, in_specs=[ANY] * na, out_specs=[ANY] * na,
        out_shape=[jax.ShapeDtypeStruct((4,) + s.shape, s.dtype) for s in shards],
        scratch_shapes=[pltpu.SemaphoreType.DMA((na, 3)), pltpu.SemaphoreType.DMA((na, 3)),
                        pltpu.SemaphoreType.DMA((na,))],
        name="gather_over_chips",
    )(*shards)


def scatter_over_chips(parts):
    na = len(parts)

    def body(*refs):
        ins, outs = refs[:na], refs[na:2 * na]
        send_sems, recv_sems, local_sems = refs[2 * na:]
        x, y, c = _coords()
        me = 2 * x + y
        copies = []
        for a in range(na):
            own = pltpu.make_async_copy(ins[a].at[me], outs[a].at[me], local_sems.at[a])
            own.start()
            copies.append(own)
        sends = []
        for a in range(na):
            for k, (p, (px, py)) in enumerate(_other_chips(x, y)):
                cp = pltpu.make_async_remote_copy(
                    src_ref=ins[a].at[p], dst_ref=outs[a].at[me], send_sem=send_sems.at[a, k],
                    recv_sem=recv_sems.at[a, k], device_id=(px, py, c), device_id_type=MESH)
                cp.start()
                sends.append(cp)
        for a in range(na):
            for k, (p, (px, py)) in enumerate(_other_chips(x, y)):
                pltpu.make_async_remote_copy(
                    src_ref=ins[a].at[p], dst_ref=outs[a].at[p], send_sem=send_sems.at[a, k],
                    recv_sem=recv_sems.at[a, k], device_id=(px, py, c), device_id_type=MESH).wait_recv()
        for cp in sends:
            cp.wait_send()
        for cp in copies:
            cp.wait()

    return pl.pallas_call(
        body, in_specs=[ANY] * na, out_specs=[ANY] * na,
        out_shape=[jax.ShapeDtypeStruct(s.shape, s.dtype) for s in parts],
        scratch_shapes=[pltpu.SemaphoreType.DMA((na, 3)), pltpu.SemaphoreType.DMA((na, 3)),
                        pltpu.SemaphoreType.DMA((na,))],
        name="scatter_over_chips",
    )(*parts)


def swap_with_sibling(arrs):
    na = len(arrs)

    def body(*refs):
        ins, outs = refs[:na], refs[na:2 * na]
        send_sems, recv_sems = refs[2 * na:]
        x, y, c = _coords()
        cps = [pltpu.make_async_remote_copy(src_ref=ins[a], dst_ref=outs[a], send_sem=send_sems.at[a],
                                            recv_sem=recv_sems.at[a], device_id=(x, y, 1 - c), device_id_type=MESH)
               for a in range(na)]
        for cp in cps:
            cp.start()
        for cp in cps:
            cp.wait()

    return pl.pallas_call(
        body, in_specs=[ANY] * na, out_specs=[ANY] * na,
        out_shape=[jax.ShapeDtypeStruct(s.shape, s.dtype) for s in arrs],
        scratch_shapes=[pltpu.SemaphoreType.DMA((na,)), pltpu.SemaphoreType.DMA((na,))],
        name="swap_with_sibling",
    )(*arrs)


def gather_over_devices(block):
    def body(in_ref, out_ref, send_sems, recv_sems, local_sem):
        x, y, c = _coords()
        me = 4 * x + 2 * y + c
        own = pltpu.make_async_copy(in_ref, out_ref.at[me], local_sem)
        own.start()
        rel = [(dx, dy, dc) for dx in (0, 1) for dy in (0, 1) for dc in (0, 1)][1:]
        peers = [(1 - x if dx else x, 1 - y if dy else y, 1 - c if dc else c) for dx, dy, dc in rel]
        sends = []
        for k, peer in enumerate(peers):
            cp = pltpu.make_async_remote_copy(src_ref=in_ref, dst_ref=out_ref.at[me], send_sem=send_sems.at[k],
                                              recv_sem=recv_sems.at[k], device_id=peer, device_id_type=MESH)
            cp.start()
            sends.append(cp)
        for k, (px, py, pc) in enumerate(peers):
            pltpu.make_async_remote_copy(src_ref=in_ref, dst_ref=out_ref.at[4 * px + 2 * py + pc],
                                         send_sem=send_sems.at[k], recv_sem=recv_sems.at[k],
                                         device_id=(px, py, pc), device_id_type=MESH).wait_recv()
        for cp in sends:
            cp.wait_send()
        own.wait()

    return pl.pallas_call(
        body, in_specs=[ANY], out_specs=ANY, out_shape=jax.ShapeDtypeStruct((8,) + block.shape, block.dtype),
        scratch_shapes=[pltpu.SemaphoreType.DMA((7,)), pltpu.SemaphoreType.DMA((7,)), pltpu.SemaphoreType.DMA],
        name="gather_over_devices",
    )(block)


def _row_tile(rows, cols, n_streams):
    budget = 24 * 1024 * 1024 // (n_streams * 2 * 4 * max(cols, 128))
    best = None
    for t in range(8, rows + 1, 8):
        if rows % t == 0 and t <= budget:
            best = t
    return best if best is not None else rows


def sum_slabs(stack):
    k, rows, cols = stack.shape
    tr = _row_tile(rows, cols, k + 1)

    def body(in_ref, out_ref):
        acc = in_ref[0]
        for j in range(1, k):
            acc = acc + in_ref[j]
        out_ref[...] = acc

    return pl.pallas_call(
        body, grid=(rows // tr,), in_specs=[pl.BlockSpec((k, tr, cols), lambda i: (0, i, 0))],
        out_specs=pl.BlockSpec((tr, cols), lambda i: (i, 0)), out_shape=jax.ShapeDtypeStruct((rows, cols), F32),
        compiler_params=pltpu.CompilerParams(dimension_semantics=("arbitrary",), vmem_limit_bytes=VMEM_LIMIT),
        name="sum_slabs",
    )(stack)


def adamw(name, g_parts, w, m, v):
    rows, cols = w.shape
    k = len(g_parts)
    tr = _row_tile(rows, cols, k + 7)
    c1 = 1.0 - ADAM_B1 ** ADAM_STEP
    c2 = 1.0 - ADAM_B2 ** ADAM_STEP

    def body(*refs):
        g_refs, (w_ref, m_ref, v_ref, g_out, d_out, m_out, v_out) = refs[:k], refs[k:]
        g = g_refs[0][...]
        for r in g_refs[1:]:
            g = g + r[...]
        m_new = ADAM_B1 * m_ref[...] + (1.0 - ADAM_B1) * g
        v_new = ADAM_B2 * v_ref[...] + (1.0 - ADAM_B2) * (g * g)
        d_out[...] = -ADAM_LR * ((m_new / c1) / (jnp.sqrt(v_new / c2) + ADAM_EPS) + ADAM_WD * w_ref[...])
        g_out[...] = g
        m_out[...] = m_new
        v_out[...] = v_new

    spec = pl.BlockSpec((tr, cols), lambda i: (i, 0))
    shape = jax.ShapeDtypeStruct((rows, cols), F32)
    return pl.pallas_call(
        body, grid=(rows // tr,), in_specs=[spec] * (k + 3), out_specs=[spec] * 4, out_shape=[shape] * 4,
        compiler_params=pltpu.CompilerParams(dimension_semantics=("arbitrary",), vmem_limit_bytes=VMEM_LIMIT),
        name=name,
    )(*g_parts, w, m, v)


_SMALL = ("norm_w", "sb_q_norm", "sb_k_norm", "gdn_a_log", "gdn_dt_bias", "gdn_out_norm", "hgrn_lb_logits",
          "hgrn_out_norm")


def _pack(arrs, lead=()):
    flat = jnp.concatenate([jnp.reshape(a, (-1,)).astype(F32) for a in list(lead) + list(arrs)])
    n = flat.shape[0]
    rows = -(-n // (8 * 128)) * 8
    return jnp.pad(flat, (0, rows * 128 - n)).reshape(rows, 128)


def _unpack(packed, shapes, n_lead=0):
    flat = packed.reshape(-1)
    out, off = [], n_lead
    for s in shapes:
        n = math.prod(s)
        out.append(flat[off:off + n].reshape(s))
        off += n
    return out


def kernel(x, meta_tokens, norm_w, w_in, sb_q_norm, sb_k_norm, gdn_conv_w, gdn_a_log, gdn_dt_bias, gdn_out_norm, hgrn_lb_logits, hgrn_out_norm, w_branch, w_out, loss_target, m_meta_tokens, m_norm_w, m_w_in, m_sb_q_norm, m_sb_k_norm, m_gdn_conv_w, m_gdn_a_log, m_gdn_dt_bias, m_gdn_out_norm, m_hgrn_lb_logits, m_hgrn_out_norm, m_w_branch, m_w_out, v_meta_tokens, v_norm_w, v_w_in, v_sb_q_norm, v_sb_k_norm, v_gdn_conv_w, v_gdn_a_log, v_gdn_dt_bias, v_gdn_out_norm, v_hgrn_lb_logits, v_hgrn_out_norm, v_w_branch, v_w_out):
    depth = norm_w.shape[0]
    small_w = dict(norm_w=norm_w, sb_q_norm=sb_q_norm, sb_k_norm=sb_k_norm, gdn_a_log=gdn_a_log,
                   gdn_dt_bias=gdn_dt_bias, gdn_out_norm=gdn_out_norm, hgrn_lb_logits=hgrn_lb_logits,
                   hgrn_out_norm=hgrn_out_norm)
    small_m = dict(zip(_SMALL, (m_norm_w, m_sb_q_norm, m_sb_k_norm, m_gdn_a_log, m_gdn_dt_bias, m_gdn_out_norm,
                                m_hgrn_lb_logits, m_hgrn_out_norm)))
    small_v = dict(zip(_SMALL, (v_norm_w, v_sb_q_norm, v_sb_k_norm, v_gdn_a_log, v_gdn_dt_bias, v_gdn_out_norm,
                                v_hgrn_lb_logits, v_hgrn_out_norm)))

    w_in_g, w_br_g, w_out_g, conv_g, meta_g = gather_over_chips(
        [w_in.astype(BF16), w_branch.astype(BF16), w_out.astype(BF16), gdn_conv_w, meta_tokens])
    w_in_full = jnp.transpose(w_in_g, (1, 2, 0, 3)).reshape(depth, D_MODEL, N_IN)
    w_br_full = jnp.transpose(w_br_g, (1, 2, 3, 0, 4)).reshape(depth, N_BRANCHES, BRANCH, D_MODEL)
    w_out_full = jnp.transpose(w_out_g, (1, 0, 2, 3)).reshape(depth, D_MODEL, D_MODEL)
    conv_full = jnp.transpose(conv_g, (1, 2, 0, 3)).reshape(depth, 4, 3 * BRANCH)
    meta_full = jnp.transpose(meta_g, (1, 0, 2)).reshape(N_META, D_MODEL)

    sq_err, grad_x, d_meta, g = device_step(
        x[0], loss_target[0], meta_full, norm_w, w_in_full, sb_q_norm, sb_k_norm, conv_full, gdn_a_log, gdn_dt_bias,
        gdn_out_norm, hgrn_lb_logits, hgrn_out_norm, w_br_full, w_out_full)

    q4 = D_MODEL // 4
    n4 = N_IN // 4
    parts = [
        jnp.transpose(g["w_in"].reshape(depth, D_MODEL, 4, n4), (2, 0, 1, 3)).reshape(4, depth * D_MODEL, n4),
        jnp.transpose(g["w_branch"].reshape(depth, N_BRANCHES, BRANCH, 4, q4), (3, 0, 1, 2, 4)).reshape(4, -1, q4),
        jnp.transpose(g["w_out"].reshape(depth, 4, q4, D_MODEL), (1, 0, 2, 3)).reshape(4, depth * q4, D_MODEL),
        jnp.transpose(g["conv_w"].reshape(depth, 4, 4, 3 * BRANCH // 4), (2, 0, 1, 3)).reshape(4, depth * 4, -1),
        jnp.transpose(d_meta.reshape(N_META, 4, q4), (1, 0, 2)),
    ]
    received = scatter_over_chips(parts)
    core_sums = [sum_slabs(r) for r in received]
    sibling_sums = swap_with_sibling(core_sums)
    c = lax.axis_index("c")
    sharded = [(w_in, m_w_in, v_w_in), (w_branch, m_w_branch, v_w_branch), (w_out, m_w_out, v_w_out),
               (gdn_conv_w, m_gdn_conv_w, v_gdn_conv_w), (meta_tokens, m_meta_tokens, v_meta_tokens)]
    big = []
    for a, (w_, m_, v_) in enumerate(sharded):
        mine, other = core_sums[a], sibling_sums[a]
        first = jnp.where(c == 0, mine, other)
        second = jnp.where(c == 0, other, mine)
        shp = mine.shape
        outs = adamw(f"adamw_{a}", [first, second], w_.reshape(shp), m_.reshape(shp), v_.reshape(shp))
        big.append([o.reshape(w_.shape) for o in outs])

    small_shapes = [small_w[n].shape for n in _SMALL]
    gs = dict(norm_w=g["norm_w"], sb_q_norm=g["sb_qn"], sb_k_norm=g["sb_kn"], gdn_a_log=g["a_log"],
              gdn_dt_bias=g["dt_bias"], gdn_out_norm=g["gd_on"], hgrn_lb_logits=g["lb_logits"],
              hgrn_out_norm=g["hg_on"])
    n_lead = 128
    lead = [jnp.pad(sq_err.reshape(1), (0, n_lead - 1))]
    packed_g = gather_over_devices(_pack([gs[n] for n in _SMALL], lead))
    zeros_lead = [jnp.zeros((n_lead,), F32)]
    pw, pm, pv = (_pack([d[n] for n in _SMALL], zeros_lead) for d in (small_w, small_m, small_v))
    sg, sd, sm, sv_ = adamw("adamw_small", [packed_g[j] for j in range(8)], pw, pm, pv)
    loss = (0.5 / D_MODEL) * sg[0, 0]
    small = [dict(zip(_SMALL, _unpack(t, small_shapes, n_lead))) for t in (sg, sd, sm, sv_)]

    order = ("meta_tokens", "norm_w", "w_in", "sb_q_norm", "sb_k_norm", "gdn_conv_w", "gdn_a_log", "gdn_dt_bias",
             "gdn_out_norm", "hgrn_lb_logits", "hgrn_out_norm", "w_branch", "w_out")
    big_idx = dict(w_in=0, w_branch=1, w_out=2, gdn_conv_w=3, meta_tokens=4)
    result = [loss, grad_x[None]]
    for kind in range(4):
        for n in order:
            result.append(big[big_idx[n]][kind] if n in big_idx else small[kind][n])
    return tuple(result)
```

```python
import functools
import math

import jax
import jax.numpy as jnp
from jax import lax
from jax.experimental import pallas as pl
from jax.experimental.pallas import tpu as pltpu

F32, BF16 = jnp.float32, jnp.bfloat16
HIGHEST = lax.Precision.HIGHEST
MESH = pl.DeviceIdType.MESH

D_MODEL = 1024
BRANCH = 512
HEAD = 128
N_HEADS = 4
N_BRANCHES = 3
CHUNK = 64
N_META = 16
FRONT = 128
PAD_FRONT = FRONT - N_META
EPS = 1e-6
SB_KEY_BLOCK = 640
SB_LANE_GROUP = 128
HALO = 8
ROW_TILE = 320
SB_Q_TILE = 320
GDN_TILE = 320
HGRN_SUB = 16
LOSS_TILE = 128
VMEM_LIMIT = 56 * 1024 * 1024

ADAM_LR, ADAM_B1, ADAM_B2, ADAM_EPS, ADAM_WD, ADAM_STEP = 0.001, 0.9, 0.999, 1e-08, 0.01, 10

_C_SB_QKV = (0, 1536)
_C_SB_Z = (1536, 2048)
_C_GD_QKV = (2048, 3584)
_C_GD_Z = (3584, 4096)
_C_GD_BA = (4096, 4104)
_C_HG_QFI = (4104, 5640)
_C_HG_Z = (5640, 6152)
_C_MIX = (6152, 9224)
N_IN = 9224


def _mxu(a):
    return a.astype(BF16)


def _dg(a, b, ca, cb, hi=False):
    dn = (((ca,), (cb,)), ((), ()))
    if hi:
        return lax.dot_general(a, b, dn, precision=HIGHEST, preferred_element_type=F32)
    return lax.dot_general(_mxu(a), _mxu(b), dn, preferred_element_type=F32)


def _make_mm(hi):
    @jax.custom_vjp
    def nn(a, b):
        return _dg(a, b, 1, 0, hi)

    @jax.custom_vjp
    def nt(a, b):
        return _dg(a, b, 1, 1, hi)

    @jax.custom_vjp
    def tn(a, b):
        return _dg(a, b, 0, 0, hi)

    nn.defvjp(lambda a, b: (nn(a, b), (a, b)), lambda r, g: (nt(g, r[1]), tn(r[0], g)))
    nt.defvjp(lambda a, b: (nt(a, b), (a, b)), lambda r, g: (nn(g, r[1]), tn(g, r[0])))
    tn.defvjp(lambda a, b: (tn(a, b), (a, b)), lambda r, g: (nt(r[1], g), nn(r[0], g)))
    return nn, nt, tn


mm_nn, mm_nt, mm_tn = _make_mm(False)
mh_nn, mh_nt, mh_tn = _make_mm(True)


@jax.custom_vjp
def _mm_w(a, w, wz):
    return _dg(a, w, 1, 0)


def _mm_w_fwd(a, w, wz):
    return _dg(a, w, 1, 0), (a, w)


def _mm_w_bwd(res, g):
    a, w = res
    return _dg(g, w, 1, 1), jnp.zeros_like(w), _dg(a, g, 0, 0)


_mm_w.defvjp(_mm_w_fwd, _mm_w_bwd)


def mmw(a, w, wz):
    return _dg(a, w, 1, 0) if wz is None else _mm_w(a, w, wz)


@functools.partial(jax.custom_vjp, nondiff_argnums=(1,))
def roll_rows(x, s):
    return pltpu.roll(x, s, 0)


roll_rows.defvjp(lambda x, s: (pltpu.roll(x, s, 0), None),
                 lambda s, r, g: (pltpu.roll(g, g.shape[0] - s, 0),))


def _sigmoid(x):
    return 0.5 * (jnp.tanh(0.5 * x) + 1.0)


def _silu(x):
    return x * _sigmoid(x)


def _softplus(x):
    return jnp.maximum(x, 0.0) + jnp.log(1.0 + jnp.exp(-jnp.abs(x)))


def _rms(x, w):
    return x * lax.rsqrt(jnp.mean(x * x, axis=-1, keepdims=True) + EPS) * w


def _l2(x):
    return x * lax.rsqrt(jnp.sum(x * x, axis=-1, keepdims=True) + EPS)


def _heads(x, fn):
    return jnp.concatenate([fn(x[:, h * HEAD:(h + 1) * HEAD]) for h in range(N_HEADS)], axis=-1)


def _valid_rows(row0, n):
    rows = row0 + lax.broadcasted_iota(jnp.int32, (n, 1), 0)
    return (rows >= PAD_FRONT).astype(F32)


def _wz(wzs, i):
    return None if wzs is None else wzs[i]


def f_sb_pre(vals, p, w, wzs, row0):
    (h,), (norm_w, qn, kn) = vals, p
    raw = mmw(_rms(h, norm_w), w[0], _wz(wzs, 0))
    q = _heads(raw[:, :BRANCH], lambda t: _rms(t, qn))
    k = _heads(raw[:, BRANCH:2 * BRANCH], lambda t: _rms(t, kn))
    return [q, k, raw[:, 2 * BRANCH:]]


def f_gd_pre(vals, p, w, wzs, row0):
    (h_ext,), (norm_w, conv_w, a_log, dt_bias) = vals, p
    tm = h_ext.shape[0] - HALO
    raw = mmw(_rms(h_ext, norm_w), w[0], _wz(wzs, 0))
    x = raw[:, :3 * BRANCH]
    y = conv_w[3:4] * x
    for i in range(3):
        y = y + conv_w[i:i + 1] * roll_rows(x, 3 - i)
    y = _silu(y[HALO:])
    gq = _heads(y[:, :BRANCH], _l2) * (HEAD ** -0.5)
    gk = _heads(y[:, BRANCH:2 * BRANCH], _l2)
    gv = y[:, 2 * BRANCH:]
    slab = raw[HALO:, 3 * BRANCH:]
    lane = lax.broadcasted_iota(jnp.int32, (1, HEAD), 1)
    beta = _sigmoid(slab) * _valid_rows(row0, tm)
    g = -jnp.exp(a_log) * _softplus(slab + dt_bias)
    bg = jnp.where(lane < N_HEADS, beta, jnp.where(lane < 2 * N_HEADS, g, 0.0))
    return [gq, gk, gv, bg]


def f_hg_pre(vals, p, w, wzs, row0):
    (h,), (norm_w, lb) = vals, p
    raw = mmw(_rms(h, norm_w), w[0], _wz(wzs, 0))
    hq = _silu(raw[:, :BRANCH])
    fp = raw[:, BRANCH:2 * BRANCH]
    forget = lb + (1.0 - lb) * _sigmoid(fp)
    hk = (1.0 - lb) * _sigmoid(-fp)
    hv = raw[:, 2 * BRANCH:] * _valid_rows(row0, h.shape[0])
    return [hq, hk, hv, jnp.log(forget)]


def _merge_u(h, o, norm_w, out_norm, w, wzs, normed):
    xn = _rms(h, norm_w)
    z = mmw(xn, w[0], _wz(wzs, 0))
    mix = mmw(xn, w[1], _wz(wzs, 1))
    if normed:
        o = _heads(o, lambda t: _rms(t, out_norm))
    return _sigmoid(mix) * mmw(o * _silu(z), w[2], _wz(wzs, 2))


def make_f_merge(normed, with_prev):
    def f(vals, p, w, wzs, row0):
        u = _merge_u(vals[0], vals[1], p[0], p[1], w, wzs, normed)
        return [vals[2] + u] if with_prev else [u]
    return f


def f_out(vals, p, w, wzs, row0):
    h, y = vals
    return [h + mmw(y, w[0], _wz(wzs, 0))]


def _full_spec(a):
    nd = a.ndim
    return pl.BlockSpec(a.shape, lambda i, _nd=nd: (0,) * _nd)


def stage_fwd(name, fn, rows, params, weights, out_widths, *, halo=False, tm=ROW_TILE):
    t_len = rows[0].shape[0]
    n = t_len // tm
    nr, npar, nw = len(rows), len(params), len(weights)

    def body(*refs):
        i = pl.program_id(0)
        refs = list(refs)
        prev_ref = refs.pop(0) if halo else None
        row_refs, refs = refs[:nr], refs[nr:]
        par_refs, refs = refs[:npar], refs[npar:]
        w_refs, out_refs = refs[:nw], refs[nw:]
        vals = [r[...] for r in row_refs]
        if halo:
            prev = jnp.where(i == 0, 0.0, prev_ref[...])
            vals[0] = jnp.concatenate([prev, vals[0]], axis=0)
        outs = fn(vals, [r[...] for r in par_refs], [r[...] for r in w_refs], None, i * tm)
        for o_ref, o in zip(out_refs, outs):
            o_ref[...] = o

    in_specs, args = [], []
    if halo:
        in_specs.append(pl.BlockSpec((HALO, rows[0].shape[1]),
                                     lambda i: (jnp.maximum(i * (tm // HALO) - 1, 0), 0)))
        args.append(rows[0])
    for r in rows:
        in_specs.append(pl.BlockSpec((tm, r.shape[1]), lambda i: (i, 0)))
        args.append(r)
    for a in list(params) + list(weights):
        in_specs.append(_full_spec(a))
        args.append(a)
    return pl.pallas_call(
        body, grid=(n,), in_specs=in_specs,
        out_specs=[pl.BlockSpec((tm, wd), lambda i: (i, 0)) for wd in out_widths],
        out_shape=[jax.ShapeDtypeStruct((t_len, wd), F32) for wd in out_widths],
        compiler_params=pltpu.CompilerParams(dimension_semantics=("arbitrary",), vmem_limit_bytes=VMEM_LIMIT),
        name=name,
    )(*args)


def stage_vjp(name, fn, rows, params, weights, cts, diff_rows, *, halo=False, acc_in=None, tm=ROW_TILE):
    t_len = rows[0].shape[0]
    n = t_len // tm
    nr, npar, nw, nct, nd = len(rows), len(params), len(weights), len(cts), len(diff_rows)
    has_acc = acc_in is not None

    def body(*refs):
        i = pl.program_id(0)
        tile = n - 1 - i
        refs = list(refs)
        prev_ref = refs.pop(0) if halo else None
        row_refs, refs = refs[:nr], refs[nr:]
        par_refs, refs = refs[:npar], refs[npar:]
        w_refs, refs = refs[:nw], refs[nw:]
        ct_refs, refs = refs[:nct], refs[nct:]
        acc_ref = refs.pop(0) if has_acc else None
        drow_refs, refs = refs[:nd], refs[nd:]
        dpar_refs, refs = refs[:npar], refs[npar:]
        dw_refs, refs = refs[:nw], refs[nw:]
        carry_ref = refs[0] if halo else None

        vals = [r[...] for r in row_refs]
        if halo:
            prev = jnp.where(tile == 0, 0.0, prev_ref[...])
            vals[0] = jnp.concatenate([prev, vals[0]], axis=0)
        pvals = [r[...] for r in par_refs]
        wvals = [r[...] for r in w_refs]
        wzs = [jnp.zeros(w.shape, F32) for w in wvals]

        def f(dvals, pv, wz):
            full = list(vals)
            for k, idx in enumerate(diff_rows):
                full[idx] = dvals[k]
            return fn(full, pv, wvals, wz, tile * tm)

        _, vjp = jax.vjp(f, [vals[k] for k in diff_rows], pvals, wzs)
        d_rows, d_par, d_w = vjp([r[...] for r in ct_refs])

        @pl.when(i == 0)
        def _():
            for r in list(dpar_refs) + list(dw_refs):
                r[...] = jnp.zeros(r.shape, F32)

        for r, g in zip(list(dpar_refs) + list(dw_refs), list(d_par) + list(d_w)):
            r[...] += g
        for k, (r, g) in enumerate(zip(drow_refs, d_rows)):
            if k == 0 and diff_rows[0] == 0:
                if halo:
                    g_ext = g
                    g = g_ext[HALO:]
                    tail = g[tm - HALO:] + jnp.where(i == 0, 0.0, carry_ref[...])
                    g = jnp.concatenate([g[:tm - HALO], tail], axis=0)
                    carry_ref[...] = g_ext[:HALO]
                if has_acc:
                    g = g + acc_ref[...]
            r[...] = g

    rev = lambda i: (n - 1 - i, 0)
    in_specs, args = [], []
    if halo:
        in_specs.append(pl.BlockSpec((HALO, rows[0].shape[1]),
                                     lambda i: (jnp.maximum((n - 1 - i) * (tm // HALO) - 1, 0), 0)))
        args.append(rows[0])
    for r in list(rows):
        in_specs.append(pl.BlockSpec((tm, r.shape[1]), rev))
        args.append(r)
    for a in list(params) + list(weights):
        in_specs.append(_full_spec(a))
        args.append(a)
    for c in cts:
        in_specs.append(pl.BlockSpec((tm, c.shape[1]), rev))
        args.append(c)
    if has_acc:
        in_specs.append(pl.BlockSpec((tm, acc_in.shape[1]), rev))
        args.append(acc_in)
    out_specs = [pl.BlockSpec((tm, rows[k].shape[1]), rev) for k in diff_rows]
    out_shape = [jax.ShapeDtypeStruct(rows[k].shape, F32) for k in diff_rows]
    for a in list(params) + list(weights):
        out_specs.append(_full_spec(a))
        out_shape.append(jax.ShapeDtypeStruct(a.shape, F32))
    scratch = [pltpu.VMEM((HALO, rows[0].shape[1]), F32)] if halo else []
    outs = pl.pallas_call(
        body, grid=(n,), in_specs=in_specs, out_specs=out_specs, out_shape=out_shape, scratch_shapes=scratch,
        compiler_params=pltpu.CompilerParams(dimension_semantics=("arbitrary",), vmem_limit_bytes=VMEM_LIMIT),
        name=name,
    )(*args)
    return outs[:nd], outs[nd:nd + npar], outs[nd + npar:]


def _sb_tile(qb, kj, t_idx, ks):
    z = _dg(qb, kj, 1, 1) * (HEAD ** -0.5)
    sp = jnp.log(1.0 + jnp.exp(-jnp.abs(z)))
    ls = jnp.minimum(z, 0.0) - sp
    if t_idx is None:
        return None, ls, ls - z
    s_idx = ks + lax.broadcasted_iota(jnp.int32, (1, SB_KEY_BLOCK), 1)
    mask = (s_idx < t_idx) & (s_idx >= PAD_FRONT)
    lk = jnp.where(mask, ls - z, 0.0)
    return mask, ls, lk


def _masked(mask, x):
    return x if mask is None else jnp.where(mask, x, 0.0)


def _walk_key_blocks(n_blocks, step, carry):
    carry = step(0, carry, True)
    last = jnp.maximum(n_blocks - 1, 1)
    carry = lax.fori_loop(1, last, lambda jj, c: step(jj, c, False), carry)
    return lax.fori_loop(last, n_blocks, lambda jj, c: step(jj, c, True), carry)


def _tri(n, kind):
    r = lax.broadcasted_iota(jnp.int32, (n, n), 0)
    c = lax.broadcasted_iota(jnp.int32, (n, n), 1)
    return {"gt": r > c, "ge": r >= c, "eq": r == c}[kind]


def _suffix_sums(x, tri, carry, inclusive):
    groups = []
    for g in reversed(range(x.shape[1] // SB_LANE_GROUP)):
        xg = x[:, g * SB_LANE_GROUP:(g + 1) * SB_LANE_GROUP]
        hi = _mxu(xg)
        r1 = xg - hi.astype(F32)
        mid = _mxu(r1)
        lo = _mxu(r1 - mid.astype(F32))
        sg = _dg(jnp.concatenate([hi, mid, lo], axis=1), tri, 1, 0)
        groups.append(sg + carry)
        carry = carry + (sg[:, 0:1] if inclusive else sg[:, 0:1] + xg[:, 0:1])
    return jnp.concatenate(groups[::-1], axis=1), carry


def sb_fwd(q, k, v, *, tq=SB_Q_TILE):
    t_len = q.shape[0]
    nq = t_len // tq
    kb = SB_KEY_BLOCK

    def body(q_ref, k_ref, v_ref, o_ref):
        i = pl.program_id(1)
        qb = _mxu(q_ref[...])
        t_idx = i * tq + lax.broadcasted_iota(jnp.int32, (tq, 1), 0)
        after = _mxu(jnp.tile(_tri(SB_LANE_GROUP, "gt").astype(F32), (3, 1)))
        o_ref[...] = jnp.zeros(o_ref.shape, F32)
        n_blocks = ((i + 1) * tq + kb - 1) // kb

        def step(jj, run, masked):
            ks = pl.multiple_of((n_blocks - 1 - jj) * kb, kb)
            kj = _mxu(k_ref[pl.ds(ks, kb), :])
            vj = _mxu(v_ref[pl.ds(ks, kb), :])
            mask, ls, lk = _sb_tile(qb, kj, t_idx if masked else None, ks)
            passed, run = _suffix_sums(lk, after, run, False)
            a = _masked(mask, jnp.exp(ls + passed))
            o_ref[...] += _dg(a, vj, 1, 0)
            return run

        _walk_key_blocks(n_blocks, step, jnp.zeros((tq, 1), F32))

    col = pl.BlockSpec((t_len, HEAD), lambda h, i: (0, h))
    tile = pl.BlockSpec((tq, HEAD), lambda h, i: (i, h))
    return pl.pallas_call(
        body, grid=(N_HEADS, nq), in_specs=[tile, col, col], out_specs=tile,
        out_shape=jax.ShapeDtypeStruct((t_len, BRANCH), F32),
        compiler_params=pltpu.CompilerParams(dimension_semantics=("arbitrary", "arbitrary"),
                                             vmem_limit_bytes=VMEM_LIMIT),
        name="sb_fwd",
    )(q, k, v)


def sb_bwd(q, k, v, o, do, *, tq=SB_Q_TILE):
    t_len = q.shape[0]
    nq = t_len // tq
    kb = SB_KEY_BLOCK
    scale = HEAD ** -0.5

    def body(q_ref, k_ref, v_ref, o_ref, do_ref, dq_ref, dk_ref, dv_ref):
        i = pl.program_id(1)

        @pl.when(i == 0)
        def _():
            dk_ref[...] = jnp.zeros(dk_ref.shape, F32)
            dv_ref[...] = jnp.zeros(dv_ref.shape, F32)

        qb = _mxu(q_ref[...])
        do_f = do_ref[...]
        dob = _mxu(do_f)
        total = jnp.sum(dob.astype(F32) * o_ref[...], axis=-1, keepdims=True)
        t_idx = i * tq + lax.broadcasted_iota(jnp.int32, (tq, 1), 0)
        after = _mxu(jnp.tile(_tri(SB_LANE_GROUP, "gt").astype(F32), (3, 1)))
        from_here = _mxu(jnp.tile(_tri(SB_LANE_GROUP, "ge").astype(F32), (3, 1)))
        dq_ref[...] = jnp.zeros(dq_ref.shape, F32)
        n_blocks = ((i + 1) * tq + kb - 1) // kb

        def step(jj, carry, masked):
            run, run_dl = carry
            ks = pl.multiple_of((n_blocks - 1 - jj) * kb, kb)
            kj = _mxu(k_ref[pl.ds(ks, kb), :])
            vj = _mxu(v_ref[pl.ds(ks, kb), :])
            mask, ls, lk = _sb_tile(qb, kj, t_idx if masked else None, ks)
            passed, run = _suffix_sums(lk, after, run, False)
            a = _masked(mask, jnp.exp(ls + passed))
            ab = _mxu(a)
            dl = ab.astype(F32) * _dg(dob, vj, 1, 1)
            from_s, run_dl = _suffix_sums(dl, from_here, run_dl, True)
            before = total - from_s
            sig = jnp.exp(ls)
            dz = (dl * (1.0 - sig) - _masked(mask, sig * before)) * scale
            dzb = _mxu(dz)
            dq_ref[...] += _dg(dzb, kj, 1, 0)
            dk_ref[pl.ds(ks, kb), :] += _dg(dzb, qb, 0, 0)
            dv_ref[pl.ds(ks, kb), :] += _dg(ab, dob, 0, 0)
            return run, run_dl

        zero = jnp.zeros((tq, 1), F32)
        _walk_key_blocks(n_blocks, step, (zero, zero))

    col = pl.BlockSpec((t_len, HEAD), lambda h, i: (0, h))
    tile = pl.BlockSpec((tq, HEAD), lambda h, i: (i, h))
    full = jax.ShapeDtypeStruct((t_len, BRANCH), F32)
    return pl.pallas_call(
        body, grid=(N_HEADS, nq), in_specs=[tile, col, col, tile, tile], out_specs=[tile, col, col],
        out_shape=[full, full, full],
        compiler_params=pltpu.CompilerParams(dimension_semantics=("arbitrary", "arbitrary"),
                                             vmem_limit_bytes=VMEM_LIMIT),
        name="sb_bwd",
    )(q, k, v, o, do)


def _bdg(a, b, ca, cb, hi=False):
    dn = (((ca,), (cb,)), ((0,), (0,)))
    if hi:
        return lax.dot_general(a, b, dn, precision=HIGHEST, preferred_element_type=F32)
    return lax.dot_general(_mxu(a), _mxu(b), dn, preferred_element_type=F32)


def _make_bmm(hi):
    @jax.custom_vjp
    def nn(a, b):
        return _bdg(a, b, 2, 1, hi)

    @jax.custom_vjp
    def nt(a, b):
        return _bdg(a, b, 2, 2, hi)

    @jax.custom_vjp
    def tn(a, b):
        return _bdg(a, b, 1, 1, hi)

    nn.defvjp(lambda a, b: (nn(a, b), (a, b)), lambda r, g: (nt(g, r[1]), tn(r[0], g)))
    nt.defvjp(lambda a, b: (nt(a, b), (a, b)), lambda r, g: (nn(g, r[1]), tn(g, r[0])))
    tn.defvjp(lambda a, b: (tn(a, b), (a, b)), lambda r, g: (nt(r[1], g), nn(r[0], g)))
    return nn, nt, tn


bm_nn, bm_nt, bm_tn = _make_bmm(False)
bh_nn, bh_nt, bh_tn = _make_bmm(True)


def gdn_tile(state, q, k, v, bg):
    c = CHUNK
    nc = q.shape[0] // c
    pairs = [(ci, h) for ci in range(nc) for h in range(N_HEADS)]
    nb = len(pairs)
    rows = lambda ci: slice(ci * c, (ci + 1) * c)
    split = lambda x: jnp.stack([x[rows(ci), h * HEAD:(h + 1) * HEAD] for ci, h in pairs])
    qs, ks, vs = split(q), split(k), split(v)
    beta = jnp.stack([bg[rows(ci), h:h + 1] for ci, h in pairs])
    g = jnp.stack([bg[rows(ci), N_HEADS + h:N_HEADS + h + 1] for ci, h in pairs])
    causal = jnp.broadcast_to(_tri(c, "ge")[None], (nb, c, c))
    strict = jnp.broadcast_to(_tri(c, "gt")[None], (nb, c, c))
    eye = jnp.broadcast_to(_tri(c, "eq")[None], (nb, c, c)).astype(F32)
    g_lanes = bh_nn(causal.astype(F32), g * jnp.ones((1, 1, HEAD), F32))
    g_col = g_lanes[:, :, :c]
    g_row = bh_nn(jnp.ones((nb, c, c), F32), eye * g_col)
    decay = jnp.where(causal, jnp.exp(jnp.where(causal, g_col - g_row, 0.0)), 0.0)
    kb = ks * beta
    m = jnp.where(strict, bm_nt(kb, ks) * decay, 0.0)
    inv = eye - m
    p = bh_nn(m, m)
    for level in range(5):
        inv = inv + bh_nn(inv, p)
        if level < 4:
            p = bh_nn(p, p)
    g1 = g_lanes[:, :, 0:1]
    g_last = g1[:, c - 1:c, :]
    u = bh_nn(inv, vs * beta)
    w = bh_nn(inv, kb * jnp.exp(g1))
    a_qk = jnp.where(causal, bm_nt(qs, ks) * decay, 0.0)
    qd = qs * jnp.exp(g1)
    kd = ks * jnp.exp(g_last - g1)
    gl = jnp.exp(g_last)
    outs = []
    for ci in range(nc):
        sl = slice(ci * N_HEADS, (ci + 1) * N_HEADS)
        v_new = u[sl] - bm_nn(w[sl], state)
        o = bm_nn(qd[sl], state) + bm_nn(a_qk[sl], v_new)
        state = state * gl[sl] + bm_tn(kd[sl], v_new)
        outs.append(jnp.concatenate([o[h] for h in range(N_HEADS)], axis=1))
    return state, jnp.concatenate(outs, axis=0)


def gdn_fwd(q, k, v, bg, *, tg=GDN_TILE):
    t_len = q.shape[0]
    n = t_len // tg

    def body(q_ref, k_ref, v_ref, bg_ref, o_ref, s_all_ref, s_ref):
        @pl.when(pl.program_id(0) == 0)
        def _():
            s_ref[...] = jnp.zeros(s_ref.shape, F32)

        s_in = s_ref[...]
        s_all_ref[0] = s_in
        s_out, o = gdn_tile(s_in, q_ref[...], k_ref[...], v_ref[...], bg_ref[...])
        o_ref[...] = o
        s_ref[...] = s_out

    row = lambda wd: pl.BlockSpec((tg, wd), lambda i: (i, 0))
    return pl.pallas_call(
        body, grid=(n,), in_specs=[row(BRANCH), row(BRANCH), row(BRANCH), row(HEAD)],
        out_specs=[row(BRANCH), pl.BlockSpec((1, N_HEADS, HEAD, HEAD), lambda i: (i, 0, 0, 0))],
        out_shape=[jax.ShapeDtypeStruct((t_len, BRANCH), F32),
                   jax.ShapeDtypeStruct((n, N_HEADS, HEAD, HEAD), F32)],
        scratch_shapes=[pltpu.VMEM((N_HEADS, HEAD, HEAD), F32)],
        compiler_params=pltpu.CompilerParams(dimension_semantics=("arbitrary",), vmem_limit_bytes=VMEM_LIMIT),
        name="gdn_fwd",
    )(q, k, v, bg)


def gdn_bwd(q, k, v, bg, s_all, do, *, tg=GDN_TILE):
    t_len = q.shape[0]
    n = t_len // tg

    def body(q_ref, k_ref, v_ref, bg_ref, s_all_ref, do_ref, dq_ref, dk_ref, dv_ref, dbg_ref, ds_ref):
        @pl.when(pl.program_id(0) == 0)
        def _():
            ds_ref[...] = jnp.zeros(ds_ref.shape, F32)

        _, vjp = jax.vjp(gdn_tile, s_all_ref[0], q_ref[...], k_ref[...], v_ref[...], bg_ref[...])
        ds, dq, dk, dv, dbg = vjp((ds_ref[...], do_ref[...]))
        ds_ref[...] = ds
        dq_ref[...] = dq
        dk_ref[...] = dk
        dv_ref[...] = dv
        dbg_ref[...] = dbg

    row = lambda wd: pl.BlockSpec((tg, wd), lambda i: (n - 1 - i, 0))
    wide, slab = jax.ShapeDtypeStruct((t_len, BRANCH), F32), jax.ShapeDtypeStruct((t_len, HEAD), F32)
    return pl.pallas_call(
        body, grid=(n,),
        in_specs=[row(BRANCH), row(BRANCH), row(BRANCH), row(HEAD),
                  pl.BlockSpec((1, N_HEADS, HEAD, HEAD), lambda i: (n - 1 - i, 0, 0, 0)), row(BRANCH)],
        out_specs=[row(BRANCH), row(BRANCH), row(BRANCH), row(HEAD)],
        out_shape=[wide, wide, wide, slab],
        scratch_shapes=[pltpu.VMEM((N_HEADS, HEAD, HEAD), F32)],
        compiler_params=pltpu.CompilerParams(dimension_semantics=("arbitrary",), vmem_limit_bytes=VMEM_LIMIT),
        name="gdn_bwd",
    )(q, k, v, bg, s_all, do)


def hgrn_chunk(state_t, q, k, v, g):
    c, sub = CHUNK, HGRN_SUB
    nsub = c // sub
    gc = mh_nn(_tri(c, "ge").astype(F32), g)
    o = mm_nt(q * jnp.exp(gc), state_t)
    blk = lambda x, b: x[b * sub:(b + 1) * sub]
    anchors = [gc[b * sub:b * sub + 1] for b in range(nsub)]
    q_anch = jnp.stack([blk(q, b) * jnp.exp(blk(gc, b) - anchors[b]) for b in range(nsub)])
    k_anch = jnp.stack([k * jnp.exp(jnp.minimum(anchors[b] - gc, 0.0)) for b in range(nsub)])
    a_far = bm_nt(q_anch, k_anch)
    a_far = jnp.concatenate([a_far[b] for b in range(nsub)], axis=0)
    r_i = lax.broadcasted_iota(jnp.int32, (c, c), 0)
    c_i = lax.broadcasted_iota(jnp.int32, (c, c), 1)
    o = o + mm_nn(jnp.where(c_i < jnp.bitwise_and(r_i, -sub), a_far, 0.0), v)
    row = lax.broadcasted_iota(jnp.int32, (c, 1), 0)
    for d in range(sub):
        kr, gr, vr = (k, gc, v) if d == 0 else (roll_rows(k, d), roll_rows(gc, d), roll_rows(v, d))
        ok = jnp.bitwise_and(row, sub - 1) >= d
        e = jnp.exp(jnp.where(ok, gc - gr, 0.0))
        a = jnp.where(ok, jnp.sum(q * kr * e, axis=-1, keepdims=True), 0.0)
        o = o + a * vr
    g_end = gc[c - 1:c, :]
    state_t = state_t * jnp.exp(g_end) + mm_tn(v, k * jnp.exp(g_end - gc))
    return state_t, o


def hgrn_fwd(q, k, v, g, *, tg=ROW_TILE):
    t_len = q.shape[0]
    n, nc = t_len // tg, tg // CHUNK

    def body(q_ref, k_ref, v_ref, g_ref, o_ref, s_all_ref, s_ref):
        @pl.when(pl.program_id(1) == 0)
        def _():
            s_ref[...] = jnp.zeros(s_ref.shape, F32)

        def step(c, _):
            r = pl.ds(pl.multiple_of(c * CHUNK, CHUNK), CHUNK)
            s_in = s_ref[...]
            s_all_ref[c, 0] = s_in
            s_out, o = hgrn_chunk(s_in, q_ref[r, :], k_ref[r, :], v_ref[r, :], g_ref[r, :])
            o_ref[r, :] = o
            s_ref[...] = s_out
            return 0

        lax.fori_loop(0, nc, step, 0)

    row = pl.BlockSpec((tg, HEAD), lambda h, i: (i, h))
    return pl.pallas_call(
        body, grid=(N_HEADS, n), in_specs=[row, row, row, row],
        out_specs=[row, pl.BlockSpec((nc, 1, HEAD, HEAD), lambda h, i: (i, h, 0, 0))],
        out_shape=[jax.ShapeDtypeStruct((t_len, BRANCH), F32),
                   jax.ShapeDtypeStruct((t_len // CHUNK, N_HEADS, HEAD, HEAD), F32)],
        scratch_shapes=[pltpu.VMEM((HEAD, HEAD), F32)],
        compiler_params=pltpu.CompilerParams(dimension_semantics=("arbitrary", "arbitrary"),
                                             vmem_limit_bytes=VMEM_LIMIT),
        name="hgrn_fwd",
    )(q, k, v, g)


def hgrn_bwd(q, k, v, g, s_all, do, *, tg=ROW_TILE):
    t_len = q.shape[0]
    n, nc = t_len // tg, tg // CHUNK

    def body(q_ref, k_ref, v_ref, g_ref, s_all_ref, do_ref, dq_ref, dk_ref, dv_ref, dg_ref, ds_ref):
        @pl.when(pl.program_id(1) == 0)
        def _():
            ds_ref[...] = jnp.zeros(ds_ref.shape, F32)

        def step(cc, _):
            c = nc - 1 - cc
            r = pl.ds(pl.multiple_of(c * CHUNK, CHUNK), CHUNK)
            _, vjp = jax.vjp(hgrn_chunk, s_all_ref[c, 0], q_ref[r, :], k_ref[r, :], v_ref[r, :], g_ref[r, :])
            ds, dq, dk, dv, dg = vjp((ds_ref[...], do_ref[r, :]))
            ds_ref[...] = ds
            dq_ref[r, :] = dq
            dk_ref[r, :] = dk
            dv_ref[r, :] = dv
            dg_ref[r, :] = dg
            return 0

        lax.fori_loop(0, nc, step, 0)

    row = pl.BlockSpec((tg, HEAD), lambda h, i: (n - 1 - i, h))
    wide = jax.ShapeDtypeStruct((t_len, BRANCH), F32)
    return pl.pallas_call(
        body, grid=(N_HEADS, n),
        in_specs=[row, row, row, row, pl.BlockSpec((nc, 1, HEAD, HEAD), lambda h, i: (n - 1 - i, h, 0, 0)), row],
        out_specs=[row, row, row, row], out_shape=[wide, wide, wide, wide],
        scratch_shapes=[pltpu.VMEM((HEAD, HEAD), F32)],
        compiler_params=pltpu.CompilerParams(dimension_semantics=("arbitrary", "arbitrary"),
                                             vmem_limit_bytes=VMEM_LIMIT),
        name="hgrn_bwd",
    )(q, k, v, g, s_all, do)


def loss_head(h, target, *, tl=LOSS_TILE):
    seq = target.shape[0]
    n = seq // tl
    off = FRONT // tl

    def body(h_ref, t_ref, sq_ref, dy_ref):
        @pl.when(pl.program_id(0) == 0)
        def _():
            sq_ref[...] = jnp.zeros(sq_ref.shape, F32)

        err = h_ref[...] - t_ref[...]
        sq_ref[...] += jnp.sum(err * err, keepdims=True)
        dy_ref[...] = err * (1.0 / D_MODEL)

    return pl.pallas_call(
        body, grid=(n,),
        in_specs=[pl.BlockSpec((tl, D_MODEL), lambda i: (i + off, 0)), pl.BlockSpec((tl, D_MODEL), lambda i: (i, 0))],
        out_specs=[pl.BlockSpec((1, 1), lambda i: (0, 0)), pl.BlockSpec((tl, D_MODEL), lambda i: (i, 0))],
        out_shape=[jax.ShapeDtypeStruct((1, 1), F32), jax.ShapeDtypeStruct((seq, D_MODEL), F32)],
        compiler_params=pltpu.CompilerParams(dimension_semantics=("arbitrary",)),
        name="loss_head",
    )(h, target)


def _pad_lanes(a, lo, n=HEAD):
    return jnp.pad(a.astype(F32), (lo, n - lo - a.shape[0])).reshape(1, n)


def _lower_bounds(logits):
    p = jax.nn.softmax(logits.astype(F32), axis=0)
    return jnp.cumsum(p, axis=0) - p[0:1]


def _layer_weights(w_in_l, w_branch_l, w_out_l):
    c = lambda r: w_in_l[:, r[0]:r[1]]
    ba = jnp.pad(c(_C_GD_BA), ((0, 0), (0, HEAD - 2 * N_HEADS)))
    mix = c(_C_MIX)
    return dict(
        sb=c(_C_SB_QKV), gd=jnp.concatenate([c(_C_GD_QKV), ba], axis=1), hg=c(_C_HG_QFI),
        z=[c(_C_SB_Z), c(_C_GD_Z), c(_C_HG_Z)],
        mix=[mix[:, b * D_MODEL:(b + 1) * D_MODEL] for b in range(N_BRANCHES)],
        br=[w_branch_l[b] for b in range(N_BRANCHES)], out=w_out_l)


def device_step(x, target, meta, norm_w, w_in, sb_qn, sb_kn, conv_w, a_log, dt_bias, gd_on, lb_logits, hg_on,
                w_branch, w_out):
    depth = norm_w.shape[0]
    lbs, lb_vjp = jax.vjp(_lower_bounds, lb_logits)
    h = jnp.concatenate([jnp.zeros((PAD_FRONT, D_MODEL), F32), meta, x], axis=0)
    row = lambda a: a.reshape(1, -1)
    saved = []
    for l in range(depth):
        w = _layer_weights(w_in[l], w_branch[l], w_out[l])
        nw = row(norm_w[l])
        p_sb = [nw, row(sb_qn[l]), row(sb_kn[l])]
        p_gd = [nw, conv_w[l], _pad_lanes(a_log[l], N_HEADS), _pad_lanes(dt_bias[l], N_HEADS)]
        p_hg = [nw, row(lbs[l])]
        out_norms = [row(sb_qn[l]), row(gd_on[l]), row(hg_on[l])]
        sq, sk, sv = stage_fwd("sb_pre", f_sb_pre, [h], p_sb, [w["sb"]], [BRANCH] * 3)
        o_sb = sb_fwd(sq, sk, sv)
        gq, gk, gv, bg = stage_fwd("gd_pre", f_gd_pre, [h], p_gd, [w["gd"]], [BRANCH] * 3 + [HEAD], halo=True)
        o_gd, gd_states = gdn_fwd(gq, gk, gv, bg)
        hq, hk, hv, hg = stage_fwd("hg_pre", f_hg_pre, [h], p_hg, [w["hg"]], [BRANCH] * 4)
        o_hg, hg_states = hgrn_fwd(hq, hk, hv, hg)
        branch_o = [o_sb, o_gd, o_hg]
        y = None
        for b in range(N_BRANCHES):
            rows = [h, branch_o[b]] + ([] if y is None else [y])
            (y,) = stage_fwd(f"merge{b}", make_f_merge(b > 0, y is not None), rows, [nw, out_norms[b]],
                             [w["z"][b], w["mix"][b], w["br"][b]], [D_MODEL])
        (h_next,) = stage_fwd("out_proj", f_out, [h, y], [], [w["out"]], [D_MODEL])
        saved.append(dict(h=h, w=w, p_sb=p_sb, p_gd=p_gd, p_hg=p_hg, out_norms=out_norms, y=y,
                          sb=(sq, sk, sv, o_sb), gd=(gq, gk, gv, bg, gd_states, o_gd),
                          hg=(hq, hk, hv, hg, hg_states, o_hg)))
        h = h_next

    sq_err, dy = loss_head(h, target)
    dh = jnp.concatenate([jnp.zeros((FRONT, D_MODEL), F32), dy], axis=0)

    g = dict(norm_w=[], w_in=[], sb_qn=[], sb_kn=[], conv_w=[], a_log=[], dt_bias=[], gd_on=[], lb=[], hg_on=[],
             w_branch=[], w_out=[])
    for l in reversed(range(depth)):
        s = saved[l]
        w, hl = s["w"], s["h"]
        nw = s["p_sb"][0]
        (dy_,), _, (d_wout,) = stage_vjp("out_proj_b", f_out, [hl, s["y"]], [], [w["out"]], [dh], [1])
        d_norm = jnp.zeros((1, D_MODEL), F32)
        d_o, d_on, d_wz, d_wmix, d_wbr = [None] * 3, [None] * 3, [None] * 3, [None] * 3, [None] * 3
        branch_o = [s["sb"][3], s["gd"][5], s["hg"][5]]
        for b in reversed(range(N_BRANCHES)):
            (dh, d_o[b]), (dn, d_on[b]), (d_wz[b], d_wmix[b], d_wbr[b]) = stage_vjp(
                f"merge{b}_b", make_f_merge(b > 0, False), [hl, branch_o[b]], [nw, s["out_norms"][b]],
                [w["z"][b], w["mix"][b], w["br"][b]], [dy_], [0, 1], acc_in=dh)
            d_norm = d_norm + dn
        hq, hk, hv, hg, hg_states, _ = s["hg"]
        d_hg = hgrn_bwd(hq, hk, hv, hg, hg_states, d_o[2])
        (dh,), (dn, d_lb), (d_whg,) = stage_vjp("hg_pre_b", f_hg_pre, [hl], s["p_hg"], [w["hg"]], list(d_hg), [0],
                                                acc_in=dh)
        d_norm = d_norm + dn
        gq, gk, gv, bg, gd_states, _ = s["gd"]
        d_gd = gdn_bwd(gq, gk, gv, bg, gd_states, d_o[1])
        (dh,), (dn, d_conv, d_alog, d_dtb), (d_wgd,) = stage_vjp(
            "gd_pre_b", f_gd_pre, [hl], s["p_gd"], [w["gd"]], list(d_gd), [0], halo=True, acc_in=dh)
        d_norm = d_norm + dn
        sq, sk, sv, o_sb = s["sb"]
        d_sb = sb_bwd(sq, sk, sv, o_sb, d_o[0])
        (dh,), (dn, d_qn, d_kn), (d_wsb,) = stage_vjp("sb_pre_b", f_sb_pre, [hl], s["p_sb"], [w["sb"]], list(d_sb),
                                                      [0], acc_in=dh)
        d_norm = d_norm + dn
        g["norm_w"].append(d_norm[0])
        g["w_in"].append(jnp.concatenate(
            [d_wsb, d_wz[0], d_wgd[:, :3 * BRANCH], d_wz[1], d_wgd[:, 3 * BRANCH:3 * BRANCH + 2 * N_HEADS],
             d_whg, d_wz[2]] + d_wmix, axis=1))
        g["sb_qn"].append(d_qn[0])
        g["sb_kn"].append(d_kn[0])
        g["conv_w"].append(d_conv)
        g["a_log"].append(d_alog[0, N_HEADS:2 * N_HEADS])
        g["dt_bias"].append(d_dtb[0, N_HEADS:2 * N_HEADS])
        g["gd_on"].append(d_on[1][0])
        g["hg_on"].append(d_on[2][0])
        g["lb"].append(d_lb[0])
        g["w_branch"].append(jnp.stack(d_wbr))
        g["w_out"].append(d_wout)
    g = {k: jnp.stack(v[::-1]) for k, v in g.items()}
    (g["lb_logits"],) = lb_vjp(g.pop("lb"))
    return sq_err, dh[FRONT:], dh[PAD_FRONT:FRONT], g


ANY = pl.BlockSpec(memory_space=pl.ANY)


def _coords():
    return lax.axis_index("x"), lax.axis_index("y"), lax.axis_index("c")


def _other_chips(x, y):
    return [(2 * px + py, (px, py)) for px, py in ((1 - x, y), (x, 1 - y), (1 - x, 1 - y))]


def gather_over_chips(shards):
    na = len(shards)

    def body(*refs):
        ins, outs = refs[:na], refs[na:2 * na]
        send_sems, recv_sems, local_sems = refs[2 * na:]
        x, y, c = _coords()
        me = 2 * x + y
        copies = []
        for a in range(na):
            own = pltpu.make_async_copy(ins[a], outs[a].at[me], local_sems.at[a])
            own.start()
            copies.append(own)
        sends = []
        for a in range(na):
            for k, (_, (px, py)) in enumerate(_other_chips(x, y)):
                cp = pltpu.make_async_remote_copy(
                    src_ref=ins[a], dst_ref=outs[a].at[me], send_sem=send_sems.at[a, k], recv_sem=recv_sems.at[a, k],
                    device_id=(px, py, c), device_id_type=MESH)
                cp.start()
                sends.append(cp)
        for a in range(na):
            for k, (p, (px, py)) in enumerate(_other_chips(x, y)):
                pltpu.make_async_remote_copy(
                    src_ref=ins[a], dst_ref=outs[a].at[p], send_sem=send_sems.at[a, k], recv_sem=recv_sems.at[a, k],
                    device_id=(px, py, c), device_id_type=MESH).wait_recv()
        for cp in sends:
            cp.wait_send()
        for cp in copies:
            cp.wait()

    return pl.pallas_call(
        body, in_specs=[ANY] * na, out_specs=[ANY] * na,
        out_shape=[jax.ShapeDtypeStruct((4,) + s.shape, s.dtype) for s in shards],
        scratch_shapes=[pltpu.SemaphoreType.DMA((na, 3)), pltpu.SemaphoreType.DMA((na, 3)),
                        pltpu.SemaphoreType.DMA((na,))],
        name="gather_over_chips",
    )(*shards)


def scatter_over_chips(parts):
    na = len(parts)

    def body(*refs):
        ins, outs = refs[:na], refs[na:2 * na]
        send_sems, recv_sems, local_sems = refs[2 * na:]
        x, y, c = _coords()
        me = 2 * x + y
        copies = []
        for a in range(na):
            own = pltpu.make_async_copy(ins[a].at[me], outs[a].at[me], local_sems.at[a])
            own.start()
            copies.append(own)
        sends = []
        for a in range(na):
            for k, (p, (px, py)) in enumerate(_other_chips(x, y)):
                cp = pltpu.make_async_remote_copy(
                    src_ref=ins[a].at[p], dst_ref=outs[a].at[me], send_sem=send_sems.at[a, k],
                    recv_sem=recv_sems.at[a, k], device_id=(px, py, c), device_id_type=MESH)
                cp.start()
                sends.append(cp)
        for a in range(na):
            for k, (p, (px, py)) in enumerate(_other_chips(x, y)):
                pltpu.make_async_remote_copy(
                    src_ref=ins[a].at[p], dst_ref=outs[a].at[p], send_sem=send_sems.at[a, k],
                    recv_sem=recv_sems.at[a, k], device_id=(px, py, c), device_id_type=MESH).wait_recv()
        for cp in sends:
            cp.wait_send()
        for cp in copies:
            cp.wait()

    return pl.pallas_call(
        body, in_specs=[ANY] * na, out_specs=[ANY] * na,
        out_shape=[jax.ShapeDtypeStruct(s.shape, s.dtype) for s in parts],
        scratch_shapes=[pltpu.SemaphoreType.DMA((na, 3)), pltpu.SemaphoreType.DMA((na, 3)),
                        pltpu.SemaphoreType.DMA((na,))],
        name="scatter_over_chips",
    )(*parts)


def swap_with_sibling(arrs):
    na = len(arrs)

    def body(*refs):
        ins, outs = refs[:na], refs[na:2 * na]
        send_sems, recv_sems = refs[2 * na:]
        x, y, c = _coords()
        cps = [pltpu.make_async_remote_copy(src_ref=ins[a], dst_ref=outs[a], send_sem=send_sems.at[a],
                                            recv_sem=recv_sems.at[a], device_id=(x, y, 1 - c), device_id_type=MESH)
               for a in range(na)]
        for cp in cps:
            cp.start()
        for cp in cps:
            cp.wait()

    return pl.pallas_call(
        body, in_specs=[ANY] * na, out_specs=[ANY] * na,
        out_shape=[jax.ShapeDtypeStruct(s.shape, s.dtype) for s in arrs],
        scratch_shapes=[pltpu.SemaphoreType.DMA((na,)), pltpu.SemaphoreType.DMA((na,))],
        name="swap_with_sibling",
    )(*arrs)


def gather_over_devices(block):
    def body(in_ref, out_ref, send_sems, recv_sems, local_sem):
        x, y, c = _coords()
        me = 4 * x + 2 * y + c
        own = pltpu.make_async_copy(in_ref, out_ref.at[me], local_sem)
        own.start()
        rel = [(dx, dy, dc) for dx in (0, 1) for dy in (0, 1) for dc in (0, 1)][1:]
        peers = [(1 - x if dx else x, 1 - y if dy else y, 1 - c if dc else c) for dx, dy, dc in rel]
        sends = []
        for k, peer in enumerate(peers):
            cp = pltpu.make_async_remote_copy(src_ref=in_ref, dst_ref=out_ref.at[me], send_sem=send_sems.at[k],
                                              recv_sem=recv_sems.at[k], device_id=peer, device_id_type=MESH)
            cp.start()
            sends.append(cp)
        for k, (px, py, pc) in enumerate(peers):
            pltpu.make_async_remote_copy(src_ref=in_ref, dst_ref=out_ref.at[4 * px + 2 * py + pc],
                                         send_sem=send_sems.at[k], recv_sem=recv_sems.at[k],
                                         device_id=(px, py, pc), device_id_type=MESH).wait_recv()
        for cp in sends:
            cp.wait_send()
        own.wait()

    return pl.pallas_call(
        body, in_specs=[ANY], out_specs=ANY, out_shape=jax.ShapeDtypeStruct((8,) + block.shape, block.dtype),
        scratch_shapes=[pltpu.SemaphoreType.DMA((7,)), pltpu.SemaphoreType.DMA((7,)), pltpu.SemaphoreType.DMA],
        name="gather_over_devices",
    )(block)


def _row_tile(rows, cols, n_streams):
    budget = 24 * 1024 * 1024 // (n_streams * 2 * 4 * max(cols, 128))
    best = None
    for t in range(8, rows + 1, 8):
        if rows % t == 0 and t <= budget:
            best = t
    return best if best is not None else rows


def sum_slabs(stack):
    k, rows, cols = stack.shape
    tr = _row_tile(rows, cols, k + 1)

    def body(in_ref, out_ref):
        acc = in_ref[0]
        for j in range(1, k):
            acc = acc + in_ref[j]
        out_ref[...] = acc

    return pl.pallas_call(
        body, grid=(rows // tr,), in_specs=[pl.BlockSpec((k, tr, cols), lambda i: (0, i, 0))],
        out_specs=pl.BlockSpec((tr, cols), lambda i: (i, 0)), out_shape=jax.ShapeDtypeStruct((rows, cols), F32),
        compiler_params=pltpu.CompilerParams(dimension_semantics=("arbitrary",), vmem_limit_bytes=VMEM_LIMIT),
        name="sum_slabs",
    )(stack)


def adamw(name, g_parts, w, m, v):
    rows, cols = w.shape
    k = len(g_parts)
    tr = _row_tile(rows, cols, k + 7)
    c1 = 1.0 - ADAM_B1 ** ADAM_STEP
    c2 = 1.0 - ADAM_B2 ** ADAM_STEP

    def body(*refs):
        g_refs, (w_ref, m_ref, v_ref, g_out, d_out, m_out, v_out) = refs[:k], refs[k:]
        g = g_refs[0][...]
        for r in g_refs[1:]:
            g = g + r[...]
        m_new = ADAM_B1 * m_ref[...] + (1.0 - ADAM_B1) * g
        v_new = ADAM_B2 * v_ref[...] + (1.0 - ADAM_B2) * (g * g)
        d_out[...] = -ADAM_LR * ((m_new / c1) / (jnp.sqrt(v_new / c2) + ADAM_EPS) + ADAM_WD * w_ref[...])
        g_out[...] = g
        m_out[...] = m_new
        v_out[...] = v_new

    spec = pl.BlockSpec((tr, cols), lambda i: (i, 0))
    shape = jax.ShapeDtypeStruct((rows, cols), F32)
    return pl.pallas_call(
        body, grid=(rows // tr,), in_specs=[spec] * (k + 3), out_specs=[spec] * 4, out_shape=[shape] * 4,
        compiler_params=pltpu.CompilerParams(dimension_semantics=("arbitrary",), vmem_limit_bytes=VMEM_LIMIT),
        name=name,
    )(*g_parts, w, m, v)


_SMALL = ("norm_w", "sb_q_norm", "sb_k_norm", "gdn_a_log", "gdn_dt_bias", "gdn_out_norm", "hgrn_lb_logits",
          "hgrn_out_norm")


def _pack(arrs, lead=()):
    flat = jnp.concatenate([jnp.reshape(a, (-1,)).astype(F32) for a in list(lead) + list(arrs)])
    n = flat.shape[0]
    rows = -(-n // (8 * 128)) * 8
    return jnp.pad(flat, (0, rows * 128 - n)).reshape(rows, 128)


def _unpack(packed, shapes, n_lead=0):
    flat = packed.reshape(-1)
    out, off = [], n_lead
    for s in shapes:
        n = math.prod(s)
        out.append(flat[off:off + n].reshape(s))
        off += n
    return out


def kernel(x, meta_tokens, norm_w, w_in, sb_q_norm, sb_k_norm, gdn_conv_w, gdn_a_log, gdn_dt_bias, gdn_out_norm, hgrn_lb_logits, hgrn_out_norm, w_branch, w_out, loss_target, m_meta_tokens, m_norm_w, m_w_in, m_sb_q_norm, m_sb_k_norm, m_gdn_conv_w, m_gdn_a_log, m_gdn_dt_bias, m_gdn_out_norm, m_hgrn_lb_logits, m_hgrn_out_norm, m_w_branch, m_w_out, v_meta_tokens, v_norm_w, v_w_in, v_sb_q_norm, v_sb_k_norm, v_gdn_conv_w, v_gdn_a_log, v_gdn_dt_bias, v_gdn_out_norm, v_hgrn_lb_logits, v_hgrn_out_norm, v_w_branch, v_w_out):
    depth = norm_w.shape[0]
    small_w = dict(norm_w=norm_w, sb_q_norm=sb_q_norm, sb_k_norm=sb_k_norm, gdn_a_log=gdn_a_log,
                   gdn_dt_bias=gdn_dt_bias, gdn_out_norm=gdn_out_norm, hgrn_lb_logits=hgrn_lb_logits,
                   hgrn_out_norm=hgrn_out_norm)
    small_m = dict(zip(_SMALL, (m_norm_w, m_sb_q_norm, m_sb_k_norm, m_gdn_a_log, m_gdn_dt_bias, m_gdn_out_norm,
                                m_hgrn_lb_logits, m_hgrn_out_norm)))
    small_v = dict(zip(_SMALL, (v_norm_w, v_sb_q_norm, v_sb_k_norm, v_gdn_a_log, v_gdn_dt_bias, v_gdn_out_norm,
                                v_hgrn_lb_logits, v_hgrn_out_norm)))

    w_in_g, w_br_g, w_out_g, conv_g, meta_g = gather_over_chips(
        [w_in.astype(BF16), w_branch.astype(BF16), w_out.astype(BF16), gdn_conv_w, meta_tokens])
    w_in_full = jnp.transpose(w_in_g, (1, 2, 0, 3)).reshape(depth, D_MODEL, N_IN)
    w_br_full = jnp.transpose(w_br_g, (1, 2, 3, 0, 4)).reshape(depth, N_BRANCHES, BRANCH, D_MODEL)
    w_out_full = jnp.transpose(w_out_g, (1, 0, 2, 3)).reshape(depth, D_MODEL, D_MODEL)
    conv_full = jnp.transpose(conv_g, (1, 2, 0, 3)).reshape(depth, 4, 3 * BRANCH)
    meta_full = jnp.transpose(meta_g, (1, 0, 2)).reshape(N_META, D_MODEL)

    sq_err, grad_x, d_meta, g = device_step(
        x[0], loss_target[0], meta_full, norm_w, w_in_full, sb_q_norm, sb_k_norm, conv_full, gdn_a_log, gdn_dt_bias,
        gdn_out_norm, hgrn_lb_logits, hgrn_out_norm, w_br_full, w_out_full)

    q4 = D_MODEL // 4
    n4 = N_IN // 4
    parts = [
        jnp.transpose(g["w_in"].reshape(depth, D_MODEL, 4, n4), (2, 0, 1, 3)).reshape(4, depth * D_MODEL, n4),
        jnp.transpose(g["w_branch"].reshape(depth, N_BRANCHES, BRANCH, 4, q4), (3, 0, 1, 2, 4)).reshape(4, -1, q4),
        jnp.transpose(g["w_out"].reshape(depth, 4, q4, D_MODEL), (1, 0, 2, 3)).reshape(4, depth * q4, D_MODEL),
        jnp.transpose(g["conv_w"].reshape(depth, 4, 4, 3 * BRANCH // 4), (2, 0, 1, 3)).reshape(4, depth * 4, -1),
        jnp.transpose(d_meta.reshape(N_META, 4, q4), (1, 0, 2)),
    ]
    received = scatter_over_chips(parts)
    core_sums = [sum_slabs(r) for r in received]
    sibling_sums = swap_with_sibling(core_sums)
    c = lax.axis_index("c")
    sharded = [(w_in, m_w_in, v_w_in), (w_branch, m_w_branch, v_w_branch), (w_out, m_w_out, v_w_out),
               (gdn_conv_w, m_gdn_conv_w, v_gdn_conv_w), (meta_tokens, m_meta_tokens, v_meta_tokens)]
    big = []
    for a, (w_, m_, v_) in enumerate(sharded):
        mine, other = core_sums[a], sibling_sums[a]
        first = jnp.where(c == 0, mine, other)
        second = jnp.where(c == 0, other, mine)
        shp = mine.shape
        outs = adamw(f"adamw_{a}", [first, second], w_.reshape(shp), m_.reshape(shp), v_.reshape(shp))
        big.append([o.reshape(w_.shape) for o in outs])

    small_shapes = [small_w[n].shape for n in _SMALL]
    gs = dict(norm_w=g["norm_w"], sb_q_norm=g["sb_qn"], sb_k_norm=g["sb_kn"], gdn_a_log=g["a_log"],
              gdn_dt_bias=g["dt_bias"], gdn_out_norm=g["gd_on"], hgrn_lb_logits=g["lb_logits"],
              hgrn_out_norm=g["hg_on"])
    n_lead = 128
    lead = [jnp.pad(sq_err.reshape(1), (0, n_lead - 1))]
    packed_g = gather_over_devices(_pack([gs[n] for n in _SMALL], lead))
    zeros_lead = [jnp.zeros((n_lead,), F32)]
    pw, pm, pv = (_pack([d[n] for n in _SMALL], zeros_lead) for d in (small_w, small_m, small_v))
    sg, sd, sm, sv_ = adamw("adamw_small", [packed_g[j] for j in range(8)], pw, pm, pv)
    loss = (0.5 / D_MODEL) * sg[0, 0]
    small = [dict(zip(_SMALL, _unpack(t, small_shapes, n_lead))) for t in (sg, sd, sm, sv_)]

    order = ("meta_tokens", "norm_w", "w_in", "sb_q_norm", "sb_k_norm", "gdn_conv_w", "gdn_a_log", "gdn_dt_bias",
             "gdn_out_norm", "hgrn_lb_logits", "hgrn_out_norm", "w_branch", "w_out")
    big_idx = dict(w_in=0, w_branch=1, w_out=2, gdn_conv_w=3, meta_tokens=4)
    result = [loss, grad_x[None]]
    for kind in range(4):
        for n in order:
            result.append(big[big_idx[n]][kind] if n in big_idx else small[kind][n])
    return tuple(result)
```

```python
import functools
import math

import jax
import jax.numpy as jnp
from jax import lax
from jax.experimental import pallas as pl
from jax.experimental.pallas import tpu as pltpu

F32, BF16 = jnp.float32, jnp.bfloat16
HIGHEST = lax.Precision.HIGHEST
MESH = pl.DeviceIdType.MESH

D_MODEL = 1024
BRANCH = 512
HEAD = 128
N_HEADS = 4
N_BRANCHES = 3
CHUNK = 64
N_META = 16
FRONT = 128
PAD_FRONT = FRONT - N_META
EPS = 1e-6
SB_KEY_BLOCK = 640
SB_LANE_GROUP = 128
HALO = 8
ROW_TILE = 320
SB_Q_TILE = 320
GDN_TILE = 320
HGRN_SUB = 16
LOSS_TILE = 128
VMEM_LIMIT = 56 * 1024 * 1024

ADAM_LR, ADAM_B1, ADAM_B2, ADAM_EPS, ADAM_WD, ADAM_STEP = 0.001, 0.9, 0.999, 1e-08, 0.01, 10

_C_SB_QKV = (0, 1536)
_C_SB_Z = (1536, 2048)
_C_GD_QKV = (2048, 3584)
_C_GD_Z = (3584, 4096)
_C_GD_BA = (4096, 4104)
_C_HG_QFI = (4104, 5640)
_C_HG_Z = (5640, 6152)
_C_MIX = (6152, 9224)
N_IN = 9224


def _mxu(a):
    return a.astype(BF16)


def _split3(x):
    x1 = _mxu(x)
    r = x - x1.astype(F32)
    x2 = _mxu(r)
    return x1, x2, _mxu(r - x2.astype(F32))


def _dg(a, b, ca, cb, hi=False):
    dn = (((ca,), (cb,)), ((), ()))
    if hi:
        return lax.dot_general(a, b, dn, precision=HIGHEST, preferred_element_type=F32)
    return lax.dot_general(_mxu(a), _mxu(b), dn, preferred_element_type=F32)


def _make_mm(hi):
    @jax.custom_vjp
    def nn(a, b):
        return _dg(a, b, 1, 0, hi)

    @jax.custom_vjp
    def nt(a, b):
        return _dg(a, b, 1, 1, hi)

    @jax.custom_vjp
    def tn(a, b):
        return _dg(a, b, 0, 0, hi)

    nn.defvjp(lambda a, b: (nn(a, b), (a, b)), lambda r, g: (nt(g, r[1]), tn(r[0], g)))
    nt.defvjp(lambda a, b: (nt(a, b), (a, b)), lambda r, g: (nn(g, r[1]), tn(g, r[0])))
    tn.defvjp(lambda a, b: (tn(a, b), (a, b)), lambda r, g: (nt(r[1], g), nn(r[0], g)))
    return nn, nt, tn


mm_nn, mm_nt, mm_tn = _make_mm(False)
mh_nn, mh_nt, mh_tn = _make_mm(True)


@jax.custom_vjp
def _mm_w(a, w, wz):
    return _dg(a, w, 1, 0)


def _mm_w_fwd(a, w, wz):
    return _dg(a, w, 1, 0), (a, w)


def _mm_w_bwd(res, g):
    a, w = res
    return _dg(g, w, 1, 1), jnp.zeros_like(w), _dg(a, g, 0, 0)


_mm_w.defvjp(_mm_w_fwd, _mm_w_bwd)


def mmw(a, w, wz):
    return _dg(a, w, 1, 0) if wz is None else _mm_w(a, w, wz)


@functools.partial(jax.custom_vjp, nondiff_argnums=(1,))
def roll_rows(x, s):
    return pltpu.roll(x, s, 0)


roll_rows.defvjp(lambda x, s: (pltpu.roll(x, s, 0), None),
                 lambda s, r, g: (pltpu.roll(g, g.shape[0] - s, 0),))


def _sigmoid(x):
    return 0.5 * (jnp.tanh(0.5 * x) + 1.0)


def _silu(x):
    return x * _sigmoid(x)


def _softplus(x):
    return jnp.maximum(x, 0.0) + jnp.log(1.0 + jnp.exp(-jnp.abs(x)))


def _rms(x, w):
    return x * lax.rsqrt(jnp.mean(x * x, axis=-1, keepdims=True) + EPS) * w


def _l2(x):
    return x * lax.rsqrt(jnp.sum(x * x, axis=-1, keepdims=True) + EPS)


def _heads(x, fn):
    return jnp.concatenate([fn(x[:, h * HEAD:(h + 1) * HEAD]) for h in range(N_HEADS)], axis=-1)


def _valid_rows(row0, n):
    rows = row0 + lax.broadcasted_iota(jnp.int32, (n, 1), 0)
    return (rows >= PAD_FRONT).astype(F32)


def _wz(wzs, i):
    return None if wzs is None else wzs[i]


def f_sb_pre(vals, p, w, wzs, row0):
    (h,), (norm_w, qn, kn) = vals, p
    raw = mmw(_rms(h, norm_w), w[0], _wz(wzs, 0))
    q = _heads(raw[:, :BRANCH], lambda t: _rms(t, qn))
    k = _heads(raw[:, BRANCH:2 * BRANCH], lambda t: _rms(t, kn))
    return [q, k, raw[:, 2 * BRANCH:]]


def f_gd_pre(vals, p, w, wzs, row0):
    (h_ext,), (norm_w, conv_w, a_log, dt_bias) = vals, p
    tm = h_ext.shape[0] - HALO
    raw = mmw(_rms(h_ext, norm_w), w[0], _wz(wzs, 0))
    x = raw[:, :3 * BRANCH]
    y = conv_w[3:4] * x
    for i in range(3):
        y = y + conv_w[i:i + 1] * roll_rows(x, 3 - i)
    y = _silu(y[HALO:])
    gq = _heads(y[:, :BRANCH], _l2) * (HEAD ** -0.5)
    gk = _heads(y[:, BRANCH:2 * BRANCH], _l2)
    gv = y[:, 2 * BRANCH:]
    slab = raw[HALO:, 3 * BRANCH:]
    lane = lax.broadcasted_iota(jnp.int32, (1, HEAD), 1)
    beta = _sigmoid(slab) * _valid_rows(row0, tm)
    g = -jnp.exp(a_log) * _softplus(slab + dt_bias)
    bg = jnp.where(lane < N_HEADS, beta, jnp.where(lane < 2 * N_HEADS, g, 0.0))
    return [gq, gk, gv, bg]


def f_hg_pre(vals, p, w, wzs, row0):
    (h,), (norm_w, lb) = vals, p
    raw = mmw(_rms(h, norm_w), w[0], _wz(wzs, 0))
    hq = _silu(raw[:, :BRANCH])
    fp = raw[:, BRANCH:2 * BRANCH]
    forget = lb + (1.0 - lb) * _sigmoid(fp)
    hk = (1.0 - lb) * _sigmoid(-fp)
    hv = raw[:, 2 * BRANCH:] * _valid_rows(row0, h.shape[0])
    return [hq, hk, hv, jnp.log(forget)]


def _merge_u(h, o, norm_w, out_norm, w, wzs, normed):
    xn = _rms(h, norm_w)
    z = mmw(xn, w[0], _wz(wzs, 0))
    mix = mmw(xn, w[1], _wz(wzs, 1))
    if normed:
        o = _heads(o, lambda t: _rms(t, out_norm))
    return _sigmoid(mix) * mmw(o * _silu(z), w[2], _wz(wzs, 2))


def make_f_merge(normed, with_prev):
    def f(vals, p, w, wzs, row0):
        u = _merge_u(vals[0], vals[1], p[0], p[1], w, wzs, normed)
        return [vals[2] + u] if with_prev else [u]
    return f


def f_out(vals, p, w, wzs, row0):
    h, y = vals
    return [h + mmw(y, w[0], _wz(wzs, 0))]


def _full_spec(a):
    nd = a.ndim
    return pl.BlockSpec(a.shape, lambda i, _nd=nd: (0,) * _nd)


def stage_fwd(name, fn, rows, params, weights, out_widths, *, halo=False, tm=ROW_TILE):
    t_len = rows[0].shape[0]
    n = t_len // tm
    nr, npar, nw = len(rows), len(params), len(weights)

    def body(*refs):
        i = pl.program_id(0)
        refs = list(refs)
        prev_ref = refs.pop(0) if halo else None
        row_refs, refs = refs[:nr], refs[nr:]
        par_refs, refs = refs[:npar], refs[npar:]
        w_refs, out_refs = refs[:nw], refs[nw:]
        vals = [r[...] for r in row_refs]
        if halo:
            prev = jnp.where(i == 0, 0.0, prev_ref[...])
            vals[0] = jnp.concatenate([prev, vals[0]], axis=0)
        outs = fn(vals, [r[...] for r in par_refs], [r[...] for r in w_refs], None, i * tm)
        for o_ref, o in zip(out_refs, outs):
            o_ref[...] = o

    in_specs, args = [], []
    if halo:
        in_specs.append(pl.BlockSpec((HALO, rows[0].shape[1]),
                                     lambda i: (jnp.maximum(i * (tm // HALO) - 1, 0), 0)))
        args.append(rows[0])
    for r in rows:
        in_specs.append(pl.BlockSpec((tm, r.shape[1]), lambda i: (i, 0)))
        args.append(r)
    for a in list(params) + list(weights):
        in_specs.append(_full_spec(a))
        args.append(a)
    return pl.pallas_call(
        body, grid=(n,), in_specs=in_specs,
        out_specs=[pl.BlockSpec((tm, wd), lambda i: (i, 0)) for wd in out_widths],
        out_shape=[jax.ShapeDtypeStruct((t_len, wd), F32) for wd in out_widths],
        compiler_params=pltpu.CompilerParams(dimension_semantics=("arbitrary",), vmem_limit_bytes=VMEM_LIMIT),
        name=name,
    )(*args)


def stage_vjp(name, fn, rows, params, weights, cts, diff_rows, *, halo=False, acc_in=None, tm=ROW_TILE):
    t_len = rows[0].shape[0]
    n = t_len // tm
    nr, npar, nw, nct, nd = len(rows), len(params), len(weights), len(cts), len(diff_rows)
    has_acc = acc_in is not None

    def body(*refs):
        i = pl.program_id(0)
        tile = n - 1 - i
        refs = list(refs)
        prev_ref = refs.pop(0) if halo else None
        row_refs, refs = refs[:nr], refs[nr:]
        par_refs, refs = refs[:npar], refs[npar:]
        w_refs, refs = refs[:nw], refs[nw:]
        ct_refs, refs = refs[:nct], refs[nct:]
        acc_ref = refs.pop(0) if has_acc else None
        drow_refs, refs = refs[:nd], refs[nd:]
        dpar_refs, refs = refs[:npar], refs[npar:]
        dw_refs, refs = refs[:nw], refs[nw:]
        carry_ref = refs[0] if halo else None

        vals = [r[...] for r in row_refs]
        if halo:
            prev = jnp.where(tile == 0, 0.0, prev_ref[...])
            vals[0] = jnp.concatenate([prev, vals[0]], axis=0)
        pvals = [r[...] for r in par_refs]
        wvals = [r[...] for r in w_refs]
        wzs = [jnp.zeros(w.shape, F32) for w in wvals]

        def f(dvals, pv, wz):
            full = list(vals)
            for k, idx in enumerate(diff_rows):
                full[idx] = dvals[k]
            return fn(full, pv, wvals, wz, tile * tm)

        _, vjp = jax.vjp(f, [vals[k] for k in diff_rows], pvals, wzs)
        d_rows, d_par, d_w = vjp([r[...] for r in ct_refs])

        @pl.when(i == 0)
        def _():
            for r in list(dpar_refs) + list(dw_refs):
                r[...] = jnp.zeros(r.shape, F32)

        for r, g in zip(list(dpar_refs) + list(dw_refs), list(d_par) + list(d_w)):
            r[...] += g
        for k, (r, g) in enumerate(zip(drow_refs, d_rows)):
            if k == 0 and diff_rows[0] == 0:
                if halo:
                    g_ext = g
                    g = g_ext[HALO:]
                    tail = g[tm - HALO:] + jnp.where(i == 0, 0.0, carry_ref[...])
                    g = jnp.concatenate([g[:tm - HALO], tail], axis=0)
                    carry_ref[...] = g_ext[:HALO]
                if has_acc:
                    g = g + acc_ref[...]
            r[...] = g

    rev = lambda i: (n - 1 - i, 0)
    in_specs, args = [], []
    if halo:
        in_specs.append(pl.BlockSpec((HALO, rows[0].shape[1]),
                                     lambda i: (jnp.maximum((n - 1 - i) * (tm // HALO) - 1, 0), 0)))
        args.append(rows[0])
    for r in list(rows):
        in_specs.append(pl.BlockSpec((tm, r.shape[1]), rev))
        args.append(r)
    for a in list(params) + list(weights):
        in_specs.append(_full_spec(a))
        args.append(a)
    for c in cts:
        in_specs.append(pl.BlockSpec((tm, c.shape[1]), rev))
        args.append(c)
    if has_acc:
        in_specs.append(pl.BlockSpec((tm, acc_in.shape[1]), rev))
        args.append(acc_in)
    out_specs = [pl.BlockSpec((tm, rows[k].shape[1]), rev) for k in diff_rows]
    out_shape = [jax.ShapeDtypeStruct(rows[k].shape, F32) for k in diff_rows]
    for a in list(params) + list(weights):
        out_specs.append(_full_spec(a))
        out_shape.append(jax.ShapeDtypeStruct(a.shape, F32))
    scratch = [pltpu.VMEM((HALO, rows[0].shape[1]), F32)] if halo else []
    outs = pl.pallas_call(
        body, grid=(n,), in_specs=in_specs, out_specs=out_specs, out_shape=out_shape, scratch_shapes=scratch,
        compiler_params=pltpu.CompilerParams(dimension_semantics=("arbitrary",), vmem_limit_bytes=VMEM_LIMIT),
        name=name,
    )(*args)
    return outs[:nd], outs[nd:nd + npar], outs[nd + npar:]


def _sb_tile(qb, kj, t_idx, ks):
    z = _dg(qb, kj, 1, 1) * (HEAD ** -0.5)
    sp = jnp.log(1.0 + jnp.exp(-jnp.abs(z)))
    ls = jnp.minimum(z, 0.0) - sp
    if t_idx is None:
        return None, ls, ls - z
    s_idx = ks + lax.broadcasted_iota(jnp.int32, (1, SB_KEY_BLOCK), 1)
    mask = (s_idx < t_idx) & (s_idx >= PAD_FRONT)
    lk = jnp.where(mask, ls - z, 0.0)
    return mask, ls, lk


def _masked(mask, x):
    return x if mask is None else jnp.where(mask, x, 0.0)


def _walk_key_blocks(n_blocks, step, carry):
    carry = step(0, carry, True)
    last = jnp.maximum(n_blocks - 1, 1)
    carry = lax.fori_loop(1, last, lambda jj, c: step(jj, c, False), carry)
    return lax.fori_loop(last, n_blocks, lambda jj, c: step(jj, c, True), carry)


def _tri(n, kind):
    r = lax.broadcasted_iota(jnp.int32, (n, n), 0)
    c = lax.broadcasted_iota(jnp.int32, (n, n), 1)
    return {"gt": r > c, "ge": r >= c, "eq": r == c}[kind]


def _suffix_sums(x, tri, carry, inclusive):
    groups = []
    for g in reversed(range(x.shape[1] // SB_LANE_GROUP)):
        xg = x[:, g * SB_LANE_GROUP:(g + 1) * SB_LANE_GROUP]
        hi = _mxu(xg)
        r1 = xg - hi.astype(F32)
        mid = _mxu(r1)
        lo = _mxu(r1 - mid.astype(F32))
        sg = _dg(jnp.concatenate([hi, mid, lo], axis=1), tri, 1, 0)
        groups.append(sg + carry)
        carry = carry + (sg[:, 0:1] if inclusive else sg[:, 0:1] + xg[:, 0:1])
    return jnp.concatenate(groups[::-1], axis=1), carry


def sb_fwd(q, k, v, *, tq=SB_Q_TILE):
    t_len = q.shape[0]
    nq = t_len // tq
    kb = SB_KEY_BLOCK

    def body(q_ref, k_ref, v_ref, o_ref):
        i = pl.program_id(1)
        qb = _mxu(q_ref[...])
        t_idx = i * tq + lax.broadcasted_iota(jnp.int32, (tq, 1), 0)
        after = _mxu(jnp.tile(_tri(SB_LANE_GROUP, "gt").astype(F32), (3, 1)))
        o_ref[...] = jnp.zeros(o_ref.shape, F32)
        n_blocks = ((i + 1) * tq + kb - 1) // kb

        def step(jj, run, masked):
            ks = pl.multiple_of((n_blocks - 1 - jj) * kb, kb)
            kj = _mxu(k_ref[pl.ds(ks, kb), :])
            vj = _mxu(v_ref[pl.ds(ks, kb), :])
            mask, ls, lk = _sb_tile(qb, kj, t_idx if masked else None, ks)
            passed, run = _suffix_sums(lk, after, run, False)
            a = _masked(mask, jnp.exp(ls + passed))
            o_ref[...] += _dg(a, vj, 1, 0)
            return run

        _walk_key_blocks(n_blocks, step, jnp.zeros((tq, 1), F32))

    col = pl.BlockSpec((t_len, HEAD), lambda h, i: (0, h))
    tile = pl.BlockSpec((tq, HEAD), lambda h, i: (i, h))
    return pl.pallas_call(
        body, grid=(N_HEADS, nq), in_specs=[tile, col, col], out_specs=tile,
        out_shape=jax.ShapeDtypeStruct((t_len, BRANCH), F32),
        compiler_params=pltpu.CompilerParams(dimension_semantics=("arbitrary", "arbitrary"),
                                             vmem_limit_bytes=VMEM_LIMIT),
        name="sb_fwd",
    )(q, k, v)


def sb_bwd(q, k, v, o, do, *, tq=SB_Q_TILE):
    t_len = q.shape[0]
    nq = t_len // tq
    kb = SB_KEY_BLOCK
    scale = HEAD ** -0.5

    def body(q_ref, k_ref, v_ref, o_ref, do_ref, dq_ref, dk_ref, dv_ref):
        i = pl.program_id(1)

        @pl.when(i == 0)
        def _():
            dk_ref[...] = jnp.zeros(dk_ref.shape, F32)
            dv_ref[...] = jnp.zeros(dv_ref.shape, F32)

        qb = _mxu(q_ref[...])
        do_f = do_ref[...]
        dob = _mxu(do_f)
        total = jnp.sum(dob.astype(F32) * o_ref[...], axis=-1, keepdims=True)
        t_idx = i * tq + lax.broadcasted_iota(jnp.int32, (tq, 1), 0)
        after = _mxu(jnp.tile(_tri(SB_LANE_GROUP, "gt").astype(F32), (3, 1)))
        from_here = _mxu(jnp.tile(_tri(SB_LANE_GROUP, "ge").astype(F32), (3, 1)))
        dq_ref[...] = jnp.zeros(dq_ref.shape, F32)
        n_blocks = ((i + 1) * tq + kb - 1) // kb

        def step(jj, carry, masked):
            run, run_dl = carry
            ks = pl.multiple_of((n_blocks - 1 - jj) * kb, kb)
            kj = _mxu(k_ref[pl.ds(ks, kb), :])
            vj = _mxu(v_ref[pl.ds(ks, kb), :])
            mask, ls, lk = _sb_tile(qb, kj, t_idx if masked else None, ks)
            passed, run = _suffix_sums(lk, after, run, False)
            a = _masked(mask, jnp.exp(ls + passed))
            ab = _mxu(a)
            dl = ab.astype(F32) * _dg(dob, vj, 1, 1)
            from_s, run_dl = _suffix_sums(dl, from_here, run_dl, True)
            before = total - from_s
            sig = jnp.exp(ls)
            dz = (dl * (1.0 - sig) - _masked(mask, sig * before)) * scale
            dzb = _mxu(dz)
            dq_ref[...] += _dg(dzb, kj, 1, 0)
            dk_ref[pl.ds(ks, kb), :] += _dg(dzb, qb, 0, 0)
            dv_ref[pl.ds(ks, kb), :] += _dg(ab, dob, 0, 0)
            return run, run_dl

        zero = jnp.zeros((tq, 1), F32)
        _walk_key_blocks(n_blocks, step, (zero, zero))

    col = pl.BlockSpec((t_len, HEAD), lambda h, i: (0, h))
    tile = pl.BlockSpec((tq, HEAD), lambda h, i: (i, h))
    full = jax.ShapeDtypeStruct((t_len, BRANCH), F32)
    return pl.pallas_call(
        body, grid=(N_HEADS, nq), in_specs=[tile, col, col, tile, tile], out_specs=[tile, col, col],
        out_shape=[full, full, full],
        compiler_params=pltpu.CompilerParams(dimension_semantics=("arbitrary", "arbitrary"),
                                             vmem_limit_bytes=VMEM_LIMIT),
        name="sb_bwd",
    )(q, k, v, o, do)


def _bdg(a, b, ca, cb, hi=False):
    dn = (((ca,), (cb,)), ((0,), (0,)))
    if hi:
        a1, a2, a3 = _split3(a)
        b1, b2, b3 = _split3(b)
        return lax.dot_general(jnp.concatenate([a1, a1, a2, a1, a2, a3], axis=ca),
                               jnp.concatenate([b1, b2, b1, b3, b2, b1], axis=cb), dn, preferred_element_type=F32)
    return lax.dot_general(_mxu(a), _mxu(b), dn, preferred_element_type=F32)


def _make_bmm(hi):
    @jax.custom_vjp
    def nn(a, b):
        return _bdg(a, b, 2, 1, hi)

    @jax.custom_vjp
    def nt(a, b):
        return _bdg(a, b, 2, 2, hi)

    @jax.custom_vjp
    def tn(a, b):
        return _bdg(a, b, 1, 1, hi)

    nn.defvjp(lambda a, b: (nn(a, b), (a, b)), lambda r, g: (nt(g, r[1]), tn(r[0], g)))
    nt.defvjp(lambda a, b: (nt(a, b), (a, b)), lambda r, g: (nn(g, r[1]), tn(g, r[0])))
    tn.defvjp(lambda a, b: (tn(a, b), (a, b)), lambda r, g: (nt(r[1], g), nn(r[0], g)))
    return nn, nt, tn


bm_nn, bm_nt, bm_tn = _make_bmm(False)
bh_nn, bh_nt, bh_tn = _make_bmm(True)


def gdn_tile(state, q, k, v, bg):
    c = CHUNK
    nc = q.shape[0] // c
    pairs = [(ci, h) for ci in range(nc) for h in range(N_HEADS)]
    nb = len(pairs)
    rows = lambda ci: slice(ci * c, (ci + 1) * c)
    split = lambda x: jnp.stack([x[rows(ci), h * HEAD:(h + 1) * HEAD] for ci, h in pairs])
    qs, ks, vs = split(q), split(k), split(v)
    beta = jnp.stack([bg[rows(ci), h:h + 1] for ci, h in pairs])
    g = jnp.stack([bg[rows(ci), N_HEADS + h:N_HEADS + h + 1] for ci, h in pairs])
    causal = jnp.broadcast_to(_tri(c, "ge")[None], (nb, c, c))
    strict = jnp.broadcast_to(_tri(c, "gt")[None], (nb, c, c))
    eye = jnp.broadcast_to(_tri(c, "eq")[None], (nb, c, c)).astype(F32)
    g_lanes = bh_nn(causal.astype(F32), g * jnp.ones((1, 1, HEAD), F32))
    g_col = g_lanes[:, :, :c]
    g_row = bh_nn(jnp.ones((nb, c, c), F32), eye * g_col)
    decay = jnp.where(causal, jnp.exp(jnp.where(causal, g_col - g_row, 0.0)), 0.0)
    kb = ks * beta
    m = jnp.where(strict, bm_nt(kb, ks) * decay, 0.0)
    inv = eye - m
    p = bh_nn(m, m)
    for level in range(5):
        inv = inv + bh_nn(inv, p)
        if level < 4:
            p = bh_nn(p, p)
    g1 = g_lanes[:, :, 0:1]
    g_last = g1[:, c - 1:c, :]
    u = bh_nn(inv, vs * beta)
    w = bh_nn(inv, kb * jnp.exp(g1))
    a_qk = jnp.where(causal, bm_nt(qs, ks) * decay, 0.0)
    qd = qs * jnp.exp(g1)
    kd = ks * jnp.exp(g_last - g1)
    gl = jnp.exp(g_last)
    outs = []
    for ci in range(nc):
        sl = slice(ci * N_HEADS, (ci + 1) * N_HEADS)
        v_new = u[sl] - bm_nn(w[sl], state)
        o = bm_nn(qd[sl], state) + bm_nn(a_qk[sl], v_new)
        state = state * gl[sl] + bm_tn(kd[sl], v_new)
        outs.append(jnp.concatenate([o[h] for h in range(N_HEADS)], axis=1))
    return state, jnp.concatenate(outs, axis=0)


def gdn_fwd(q, k, v, bg, *, tg=GDN_TILE):
    t_len = q.shape[0]
    n = t_len // tg

    def body(q_ref, k_ref, v_ref, bg_ref, o_ref, s_all_ref, s_ref):
        @pl.when(pl.program_id(0) == 0)
        def _():
            s_ref[...] = jnp.zeros(s_ref.shape, F32)

        s_in = s_ref[...]
        s_all_ref[0] = s_in
        s_out, o = gdn_tile(s_in, q_ref[...], k_ref[...], v_ref[...], bg_ref[...])
        o_ref[...] = o
        s_ref[...] = s_out

    row = lambda wd: pl.BlockSpec((tg, wd), lambda i: (i, 0))
    return pl.pallas_call(
        body, grid=(n,), in_specs=[row(BRANCH), row(BRANCH), row(BRANCH), row(HEAD)],
        out_specs=[row(BRANCH), pl.BlockSpec((1, N_HEADS, HEAD, HEAD), lambda i: (i, 0, 0, 0))],
        out_shape=[jax.ShapeDtypeStruct((t_len, BRANCH), F32),
                   jax.ShapeDtypeStruct((n, N_HEADS, HEAD, HEAD), F32)],
        scratch_shapes=[pltpu.VMEM((N_HEADS, HEAD, HEAD), F32)],
        compiler_params=pltpu.CompilerParams(dimension_semantics=("arbitrary",), vmem_limit_bytes=VMEM_LIMIT),
        name="gdn_fwd",
    )(q, k, v, bg)


def gdn_bwd(q, k, v, bg, s_all, do, *, tg=GDN_TILE):
    t_len = q.shape[0]
    n = t_len // tg

    def body(q_ref, k_ref, v_ref, bg_ref, s_all_ref, do_ref, dq_ref, dk_ref, dv_ref, dbg_ref, ds_ref):
        @pl.when(pl.program_id(0) == 0)
        def _():
            ds_ref[...] = jnp.zeros(ds_ref.shape, F32)

        _, vjp = jax.vjp(gdn_tile, s_all_ref[0], q_ref[...], k_ref[...], v_ref[...], bg_ref[...])
        ds, dq, dk, dv, dbg = vjp((ds_ref[...], do_ref[...]))
        ds_ref[...] = ds
        dq_ref[...] = dq
        dk_ref[...] = dk
        dv_ref[...] = dv
        dbg_ref[...] = dbg

    row = lambda wd: pl.BlockSpec((tg, wd), lambda i: (n - 1 - i, 0))
    wide, slab = jax.ShapeDtypeStruct((t_len, BRANCH), F32), jax.ShapeDtypeStruct((t_len, HEAD), F32)
    return pl.pallas_call(
        body, grid=(n,),
        in_specs=[row(BRANCH), row(BRANCH), row(BRANCH), row(HEAD),
                  pl.BlockSpec((1, N_HEADS, HEAD, HEAD), lambda i: (n - 1 - i, 0, 0, 0)), row(BRANCH)],
        out_specs=[row(BRANCH), row(BRANCH), row(BRANCH), row(HEAD)],
        out_shape=[wide, wide, wide, slab],
        scratch_shapes=[pltpu.VMEM((N_HEADS, HEAD, HEAD), F32)],
        compiler_params=pltpu.CompilerParams(dimension_semantics=("arbitrary",), vmem_limit_bytes=VMEM_LIMIT),
        name="gdn_bwd",
    )(q, k, v, bg, s_all, do)


def hgrn_chunk(state_t, q, k, v, g):
    c, sub = CHUNK, HGRN_SUB
    nsub = c // sub
    gc = mh_nn(_tri(c, "ge").astype(F32), g)
    o = mm_nt(q * jnp.exp(gc), state_t)
    blk = lambda x, b: x[b * sub:(b + 1) * sub]
    anchors = [gc[b * sub:b * sub + 1] for b in range(nsub)]
    q_anch = jnp.stack([blk(q, b) * jnp.exp(blk(gc, b) - anchors[b]) for b in range(nsub)])
    k_anch = jnp.stack([k * jnp.exp(jnp.minimum(anchors[b] - gc, 0.0)) for b in range(nsub)])
    a_far = bm_nt(q_anch, k_anch)
    a_far = jnp.concatenate([a_far[b] for b in range(nsub)], axis=0)
    r_i = lax.broadcasted_iota(jnp.int32, (c, c), 0)
    c_i = lax.broadcasted_iota(jnp.int32, (c, c), 1)
    o = o + mm_nn(jnp.where(c_i < jnp.bitwise_and(r_i, -sub), a_far, 0.0), v)
    row = lax.broadcasted_iota(jnp.int32, (c, 1), 0)
    for d in range(sub):
        kr, gr, vr = (k, gc, v) if d == 0 else (roll_rows(k, d), roll_rows(gc, d), roll_rows(v, d))
        ok = jnp.bitwise_and(row, sub - 1) >= d
        e = jnp.exp(jnp.where(ok, gc - gr, 0.0))
        a = jnp.where(ok, jnp.sum(q * kr * e, axis=-1, keepdims=True), 0.0)
        o = o + a * vr
    g_end = gc[c - 1:c, :]
    state_t = state_t * jnp.exp(g_end) + mm_tn(v, k * jnp.exp(g_end - gc))
    return state_t, o


def hgrn_fwd(q, k, v, g, *, tg=ROW_TILE):
    t_len = q.shape[0]
    n, nc = t_len // tg, tg // CHUNK

    def body(q_ref, k_ref, v_ref, g_ref, o_ref, s_all_ref, s_ref):
        @pl.when(pl.program_id(1) == 0)
        def _():
            s_ref[...] = jnp.zeros(s_ref.shape, F32)

        def step(c, _):
            r = pl.ds(pl.multiple_of(c * CHUNK, CHUNK), CHUNK)
            s_in = s_ref[...]
            s_all_ref[c, 0] = s_in
            s_out, o = hgrn_chunk(s_in, q_ref[r, :], k_ref[r, :], v_ref[r, :], g_ref[r, :])
            o_ref[r, :] = o
            s_ref[...] = s_out
            return 0

        lax.fori_loop(0, nc, step, 0)

    row = pl.BlockSpec((tg, HEAD), lambda h, i: (i, h))
    return pl.pallas_call(
        body, grid=(N_HEADS, n), in_specs=[row, row, row, row],
        out_specs=[row, pl.BlockSpec((nc, 1, HEAD, HEAD), lambda h, i: (i, h, 0, 0))],
        out_shape=[jax.ShapeDtypeStruct((t_len, BRANCH), F32),
                   jax.ShapeDtypeStruct((t_len // CHUNK, N_HEADS, HEAD, HEAD), F32)],
        scratch_shapes=[pltpu.VMEM((HEAD, HEAD), F32)],
        compiler_params=pltpu.CompilerParams(dimension_semantics=("arbitrary", "arbitrary"),
                                             vmem_limit_bytes=VMEM_LIMIT),
        name="hgrn_fwd",
    )(q, k, v, g)


def hgrn_bwd(q, k, v, g, s_all, do, *, tg=ROW_TILE):
    t_len = q.shape[0]
    n, nc = t_len // tg, tg // CHUNK

    def body(q_ref, k_ref, v_ref, g_ref, s_all_ref, do_ref, dq_ref, dk_ref, dv_ref, dg_ref, ds_ref):
        @pl.when(pl.program_id(1) == 0)
        def _():
            ds_ref[...] = jnp.zeros(ds_ref.shape, F32)

        def step(cc, _):
            c = nc - 1 - cc
            r = pl.ds(pl.multiple_of(c * CHUNK, CHUNK), CHUNK)
            _, vjp = jax.vjp(hgrn_chunk, s_all_ref[c, 0], q_ref[r, :], k_ref[r, :], v_ref[r, :], g_ref[r, :])
            ds, dq, dk, dv, dg = vjp((ds_ref[...], do_ref[r, :]))
            ds_ref[...] = ds
            dq_ref[r, :] = dq
            dk_ref[r, :] = dk
            dv_ref[r, :] = dv
            dg_ref[r, :] = dg
            return 0

        lax.fori_loop(0, nc, step, 0)

    row = pl.BlockSpec((tg, HEAD), lambda h, i: (n - 1 - i, h))
    wide = jax.ShapeDtypeStruct((t_len, BRANCH), F32)
    return pl.pallas_call(
        body, grid=(N_HEADS, n),
        in_specs=[row, row, row, row, pl.BlockSpec((nc, 1, HEAD, HEAD), lambda h, i: (n - 1 - i, h, 0, 0)), row],
        out_specs=[row, row, row, row], out_shape=[wide, wide, wide, wide],
        scratch_shapes=[pltpu.VMEM((HEAD, HEAD), F32)],
        compiler_params=pltpu.CompilerParams(dimension_semantics=("arbitrary", "arbitrary"),
                                             vmem_limit_bytes=VMEM_LIMIT),
        name="hgrn_bwd",
    )(q, k, v, g, s_all, do)


def loss_head(h, target, *, tl=LOSS_TILE):
    seq = target.shape[0]
    n = seq // tl
    off = FRONT // tl

    def body(h_ref, t_ref, sq_ref, dy_ref):
        @pl.when(pl.program_id(0) == 0)
        def _():
            sq_ref[...] = jnp.zeros(sq_ref.shape, F32)

        err = h_ref[...] - t_ref[...]
        sq_ref[...] += jnp.sum(err * err, keepdims=True)
        dy_ref[...] = err * (1.0 / D_MODEL)

    return pl.pallas_call(
        body, grid=(n,),
        in_specs=[pl.BlockSpec((tl, D_MODEL), lambda i: (i + off, 0)), pl.BlockSpec((tl, D_MODEL), lambda i: (i, 0))],
        out_specs=[pl.BlockSpec((1, 1), lambda i: (0, 0)), pl.BlockSpec((tl, D_MODEL), lambda i: (i, 0))],
        out_shape=[jax.ShapeDtypeStruct((1, 1), F32), jax.ShapeDtypeStruct((seq, D_MODEL), F32)],
        compiler_params=pltpu.CompilerParams(dimension_semantics=("arbitrary",)),
        name="loss_head",
    )(h, target)


def _pad_lanes(a, lo, n=HEAD):
    return jnp.pad(a.astype(F32), (lo, n - lo - a.shape[0])).reshape(1, n)


def _lower_bounds(logits):
    p = jax.nn.softmax(logits.astype(F32), axis=0)
    return jnp.cumsum(p, axis=0) - p[0:1]


def _layer_weights(w_in_l, w_branch_l, w_out_l):
    c = lambda r: w_in_l[:, r[0]:r[1]]
    ba = jnp.pad(c(_C_GD_BA), ((0, 0), (0, HEAD - 2 * N_HEADS)))
    mix = c(_C_MIX)
    return dict(
        sb=c(_C_SB_QKV), gd=jnp.concatenate([c(_C_GD_QKV), ba], axis=1), hg=c(_C_HG_QFI),
        z=[c(_C_SB_Z), c(_C_GD_Z), c(_C_HG_Z)],
        mix=[mix[:, b * D_MODEL:(b + 1) * D_MODEL] for b in range(N_BRANCHES)],
        br=[w_branch_l[b] for b in range(N_BRANCHES)], out=w_out_l)


def device_step(x, target, meta, norm_w, w_in, sb_qn, sb_kn, conv_w, a_log, dt_bias, gd_on, lb_logits, hg_on,
                w_branch, w_out):
    depth = norm_w.shape[0]
    lbs, lb_vjp = jax.vjp(_lower_bounds, lb_logits)
    h = jnp.concatenate([jnp.zeros((PAD_FRONT, D_MODEL), F32), meta, x], axis=0)
    row = lambda a: a.reshape(1, -1)
    saved = []
    for l in range(depth):
        w = _layer_weights(w_in[l], w_branch[l], w_out[l])
        nw = row(norm_w[l])
        p_sb = [nw, row(sb_qn[l]), row(sb_kn[l])]
        p_gd = [nw, conv_w[l], _pad_lanes(a_log[l], N_HEADS), _pad_lanes(dt_bias[l], N_HEADS)]
        p_hg = [nw, row(lbs[l])]
        out_norms = [row(sb_qn[l]), row(gd_on[l]), row(hg_on[l])]
        sq, sk, sv = stage_fwd("sb_pre", f_sb_pre, [h], p_sb, [w["sb"]], [BRANCH] * 3)
        o_sb = sb_fwd(sq, sk, sv)
        gq, gk, gv, bg = stage_fwd("gd_pre", f_gd_pre, [h], p_gd, [w["gd"]], [BRANCH] * 3 + [HEAD], halo=True)
        o_gd, gd_states = gdn_fwd(gq, gk, gv, bg)
        hq, hk, hv, hg = stage_fwd("hg_pre", f_hg_pre, [h], p_hg, [w["hg"]], [BRANCH] * 4)
        o_hg, hg_states = hgrn_fwd(hq, hk, hv, hg)
        branch_o = [o_sb, o_gd, o_hg]
        y = None
        for b in range(N_BRANCHES):
            rows = [h, branch_o[b]] + ([] if y is None else [y])
            (y,) = stage_fwd(f"merge{b}", make_f_merge(b > 0, y is not None), rows, [nw, out_norms[b]],
                             [w["z"][b], w["mix"][b], w["br"][b]], [D_MODEL])
        (h_next,) = stage_fwd("out_proj", f_out, [h, y], [], [w["out"]], [D_MODEL])
        saved.append(dict(h=h, w=w, p_sb=p_sb, p_gd=p_gd, p_hg=p_hg, out_norms=out_norms, y=y,
                          sb=(sq, sk, sv, o_sb), gd=(gq, gk, gv, bg, gd_states, o_gd),
                          hg=(hq, hk, hv, hg, hg_states, o_hg)))
        h = h_next

    sq_err, dy = loss_head(h, target)
    dh = jnp.concatenate([jnp.zeros((FRONT, D_MODEL), F32), dy], axis=0)

    g = dict(norm_w=[], w_in=[], sb_qn=[], sb_kn=[], conv_w=[], a_log=[], dt_bias=[], gd_on=[], lb=[], hg_on=[],
             w_branch=[], w_out=[])
    for l in reversed(range(depth)):
        s = saved[l]
        w, hl = s["w"], s["h"]
        nw = s["p_sb"][0]
        (dy_,), _, (d_wout,) = stage_vjp("out_proj_b", f_out, [hl, s["y"]], [], [w["out"]], [dh], [1])
        d_norm = jnp.zeros((1, D_MODEL), F32)
        d_o, d_on, d_wz, d_wmix, d_wbr = [None] * 3, [None] * 3, [None] * 3, [None] * 3, [None] * 3
        branch_o = [s["sb"][3], s["gd"][5], s["hg"][5]]
        for b in reversed(range(N_BRANCHES)):
            (dh, d_o[b]), (dn, d_on[b]), (d_wz[b], d_wmix[b], d_wbr[b]) = stage_vjp(
                f"merge{b}_b", make_f_merge(b > 0, False), [hl, branch_o[b]], [nw, s["out_norms"][b]],
                [w["z"][b], w["mix"][b], w["br"][b]], [dy_], [0, 1], acc_in=dh)
            d_norm = d_norm + dn
        hq, hk, hv, hg, hg_states, _ = s["hg"]
        d_hg = hgrn_bwd(hq, hk, hv, hg, hg_states, d_o[2])
        (dh,), (dn, d_lb), (d_whg,) = stage_vjp("hg_pre_b", f_hg_pre, [hl], s["p_hg"], [w["hg"]], list(d_hg), [0],
                                                acc_in=dh)
        d_norm = d_norm + dn
        gq, gk, gv, bg, gd_states, _ = s["gd"]
        d_gd = gdn_bwd(gq, gk, gv, bg, gd_states, d_o[1])
        (dh,), (dn, d_conv, d_alog, d_dtb), (d_wgd,) = stage_vjp(
            "gd_pre_b", f_gd_pre, [hl], s["p_gd"], [w["gd"]], list(d_gd), [0], halo=True, acc_in=dh)
        d_norm = d_norm + dn
        sq, sk, sv, o_sb = s["sb"]
        d_sb = sb_bwd(sq, sk, sv, o_sb, d_o[0])
        (dh,), (dn, d_qn, d_kn), (d_wsb,) = stage_vjp("sb_pre_b", f_sb_pre, [hl], s["p_sb"], [w["sb"]], list(d_sb),
                                                      [0], acc_in=dh)
        d_norm = d_norm + dn
        g["norm_w"].append(d_norm[0])
        g["w_in"].append(jnp.concatenate(
            [d_wsb, d_wz[0], d_wgd[:, :3 * BRANCH], d_wz[1], d_wgd[:, 3 * BRANCH:3 * BRANCH + 2 * N_HEADS],
             d_whg, d_wz[2]] + d_wmix, axis=1))
        g["sb_qn"].append(d_qn[0])
        g["sb_kn"].append(d_kn[0])
        g["conv_w"].append(d_conv)
        g["a_log"].append(d_alog[0, N_HEADS:2 * N_HEADS])
        g["dt_bias"].append(d_dtb[0, N_HEADS:2 * N_HEADS])
        g["gd_on"].append(d_on[1][0])
        g["hg_on"].append(d_on[2][0])
        g["lb"].append(d_lb[0])
        g["w_branch"].append(jnp.stack(d_wbr))
        g["w_out"].append(d_wout)
    g = {k: jnp.stack(v[::-1]) for k, v in g.items()}
    (g["lb_logits"],) = lb_vjp(g.pop("lb"))
    return sq_err, dh[FRONT:], dh[PAD_FRONT:FRONT], g


ANY = pl.BlockSpec(memory_space=pl.ANY)


def _coords():
    return lax.axis_index("x"), lax.axis_index("y"), lax.axis_index("c")


def _other_chips(x, y):
    return [(2 * px + py, (px, py)) for px, py in ((1 - x, y), (x, 1 - y), (1 - x, 1 - y))]


def gather_over_chips(shards):
    na = len(shards)

    def body(*refs):
        ins, outs = refs[:na], refs[na:2 * na]
        send_sems, recv_sems, local_sems = refs[2 * na:]
        x, y, c = _coords()
        me = 2 * x + y
        copies = []
        for a in range(na):
            own = pltpu.make_async_copy(ins[a], outs[a].at[me], local_sems.at[a])
            own.start()
            copies.append(own)
        sends = []
        for a in range(na):
            for k, (_, (px, py)) in enumerate(_other_chips(x, y)):
                cp = pltpu.make_async_remote_copy(
                    src_ref=ins[a], dst_ref=outs[a].at[me], send_sem=send_sems.at[a, k], recv_sem=recv_sems.at[a, k],
                    device_id=(px, py, c), device_id_type=MESH)
                cp.start()
                sends.append(cp)
        for a in range(na):
            for k, (p, (px, py)) in enumerate(_other_chips(x, y)):
                pltpu.make_async_remote_copy(
                    src_ref=ins[a], dst_ref=outs[a].at[p], send_sem=send_sems.at[a, k], recv_sem=recv_sems.at[a, k],
                    device_id=(px, py, c), device_id_type=MESH).wait_recv()
        for cp in sends:
            cp.wait_send()
        for cp in copies:
            cp.wait()

    return pl.pallas_call(
        body, in_specs=[ANY] * na, out_specs=[ANY] * na,
        out_shape=[jax.ShapeDtypeStruct((4,) + s.shape, s.dtype) for s in shards],
        scratch_shapes=[pltpu.SemaphoreType.DMA((na, 3)), pltpu.SemaphoreType.DMA((na, 3)),
                        pltpu.SemaphoreType.DMA((na,))],
        name="gather_over_chips",
    )(*shards)


def scatter_over_chips(parts):
    na = len(parts)

    def body(*refs):
        ins, outs = refs[:na], refs[na:2 * na]
        send_sems, recv_sems, local_sems = refs[2 * na:]
        x, y, c = _coords()
        me = 2 * x + y
        copies = []
        for a in range(na):
            own = pltpu.make_async_copy(ins[a].at[me], outs[a].at[me], local_sems.at[a])
            own.start()
            copies.append(own)
        sends = []
        for a in range(na):
            for k, (p, (px, py)) in enumerate(_other_chips(x, y)):
                cp = pltpu.make_async_remote_copy(
                    src_ref=ins[a].at[p], dst_ref=outs[a].at[me], send_sem=send_sems.at[a, k],
                    recv_sem=recv_sems.at[a, k], device_id=(px, py, c), device_id_type=MESH)
                cp.start()
                sends.append(cp)
        for a in range(na):
            for k, (p, (px, py)) in enumerate(_other_chips(x, y)):
                pltpu.make_async_remote_copy(
                    src_ref=ins[a].at[p], dst_ref=outs[a].at[p], send_sem=send_sems.at[a, k],
                    recv_sem=recv_sems.at[a, k], device_id=(px, py, c), device_id_type=MESH).wait_recv()
        for cp in sends:
            cp.wait_send()
        for cp in copies:
            cp.wait()

    return pl.pallas_call(
        body, in_specs=[ANY] * na, out_specs=[ANY] * na,
        out_shape=[jax.ShapeDtypeStruct(s.shape, s.dtype) for s in parts],
        scratch_shapes=[pltpu.SemaphoreType.DMA((na, 3)), pltpu.SemaphoreType.DMA((na, 3)),
                        pltpu.SemaphoreType.DMA((na,))],
        name="scatter_over_chips",
    )(*parts)


def swap_with_sibling(arrs):
    na = len(arrs)

    def body(*refs):
        ins, outs = refs[:na], refs[na:2 * na]
        send_sems, recv_sems = refs[2 * na:]
        x, y, c = _coords()
        cps = [pltpu.make_async_remote_copy(src_ref=ins[a], dst_ref=outs[a], send_sem=send_sems.at[a],
                                            recv_sem=recv_sems.at[a], device_id=(x, y, 1 - c), device_id_type=MESH)
               for a in range(na)]
        for cp in cps:
            cp.start()
        for cp in cps:
            cp.wait()

    return pl.pallas_call(
        body, in_specs=[ANY] * na, out_specs=[ANY] * na,
        out_shape=[jax.ShapeDtypeStruct(s.shape, s.dtype) for s in arrs],
        scratch_shapes=[pltpu.SemaphoreType.DMA((na,)), pltpu.SemaphoreType.DMA((na,))],
        name="swap_with_sibling",
    )(*arrs)


def gather_over_devices(block):
    def body(in_ref, out_ref, send_sems, recv_sems, local_sem):
        x, y, c = _coords()
        me = 4 * x + 2 * y + c
        own = pltpu.make_async_copy(in_ref, out_ref.at[me], local_sem)
        own.start()
        rel = [(dx, dy, dc) for dx in (0, 1) for dy in (0, 1) for dc in (0, 1)][1:]
        peers = [(1 - x if dx else x, 1 - y if dy else y, 1 - c if dc else c) for dx, dy, dc in rel]
        sends = []
        for k, peer in enumerate(peers):
            cp = pltpu.make_async_remote_copy(src_ref=in_ref, dst_ref=out_ref.at[me], send_sem=send_sems.at[k],
                                              recv_sem=recv_sems.at[k], device_id=peer, device_id_type=MESH)
            cp.start()
            sends.append(cp)
        for k, (px, py, pc) in enumerate(peers):
            pltpu.make_async_remote_copy(src_ref=in_ref, dst_ref=out_ref.at[4 * px + 2 * py + pc],
                                         send_sem=send_sems.at[k], recv_sem=recv_sems.at[k],
                                         device_id=(px, py, pc), device_id_type=MESH).wait_recv()
        for cp in sends:
            cp.wait_send()
        own.wait()

    return pl.pallas_call(
        body, in_specs=[ANY], out_specs=ANY, out_shape=jax.ShapeDtypeStruct((8,) + block.shape, block.dtype),
        scratch_shapes=[pltpu.SemaphoreType.DMA((7,)), pltpu.SemaphoreType.DMA((7,)), pltpu.SemaphoreType.DMA],
        name="gather_over_devices",
    )(block)


def _row_tile(rows, cols, n_streams):
    budget = 24 * 1024 * 1024 // (n_streams * 2 * 4 * max(cols, 128))
    best = None
    for t in range(16, rows + 1, 16):
        if rows % t == 0 and t <= budget:
            best = t
    return best if best is not None else rows


def sum_slabs(stack):
    k, rows, cols = stack.shape
    tr = _row_tile(rows, cols, k + 1)

    def body(in_ref, out_ref):
        acc = in_ref[0].astype(F32)
        for j in range(1, k):
            acc = acc + in_ref[j].astype(F32)
        out_ref[...] = acc

    return pl.pallas_call(
        body, grid=(rows // tr,), in_specs=[pl.BlockSpec((k, tr, cols), lambda i: (0, i, 0))],
        out_specs=pl.BlockSpec((tr, cols), lambda i: (i, 0)), out_shape=jax.ShapeDtypeStruct((rows, cols), F32),
        compiler_params=pltpu.CompilerParams(dimension_semantics=("arbitrary",), vmem_limit_bytes=VMEM_LIMIT),
        name="sum_slabs",
    )(stack)


def adamw(name, g_parts, w, m, v):
    rows, cols = w.shape
    k = len(g_parts)
    tr = _row_tile(rows, cols, k + 7)
    c1 = 1.0 - ADAM_B1 ** ADAM_STEP
    c2 = 1.0 - ADAM_B2 ** ADAM_STEP

    def body(*refs):
        g_refs, (w_ref, m_ref, v_ref, g_out, d_out, m_out, v_out) = refs[:k], refs[k:]
        g = g_refs[0][...]
        for r in g_refs[1:]:
            g = g + r[...]
        m_new = ADAM_B1 * m_ref[...] + (1.0 - ADAM_B1) * g
        v_new = ADAM_B2 * v_ref[...] + (1.0 - ADAM_B2) * (g * g)
        d_out[...] = -ADAM_LR * ((m_new / c1) / (jnp.sqrt(v_new / c2) + ADAM_EPS) + ADAM_WD * w_ref[...])
        g_out[...] = g
        m_out[...] = m_new
        v_out[...] = v_new

    spec = pl.BlockSpec((tr, cols), lambda i: (i, 0))
    shape = jax.ShapeDtypeStruct((rows, cols), F32)
    return pl.pallas_call(
        body, grid=(rows // tr,), in_specs=[spec] * (k + 3), out_specs=[spec] * 4, out_shape=[shape] * 4,
        compiler_params=pltpu.CompilerParams(dimension_semantics=("arbitrary",), vmem_limit_bytes=VMEM_LIMIT),
        name=name,
    )(*g_parts, w, m, v)


_SMALL = ("norm_w", "sb_q_norm", "sb_k_norm", "gdn_a_log", "gdn_dt_bias", "gdn_out_norm", "hgrn_lb_logits",
          "hgrn_out_norm")


def _pack(arrs, lead=()):
    flat = jnp.concatenate([jnp.reshape(a, (-1,)).astype(F32) for a in list(lead) + list(arrs)])
    n = flat.shape[0]
    rows = -(-n // (8 * 128)) * 8
    return jnp.pad(flat, (0, rows * 128 - n)).reshape(rows, 128)


def _unpack(packed, shapes, n_lead=0):
    flat = packed.reshape(-1)
    out, off = [], n_lead
    for s in shapes:
        n = math.prod(s)
        out.append(flat[off:off + n].reshape(s))
        off += n
    return out


def kernel(x, meta_tokens, norm_w, w_in, sb_q_norm, sb_k_norm, gdn_conv_w, gdn_a_log, gdn_dt_bias, gdn_out_norm, hgrn_lb_logits, hgrn_out_norm, w_branch, w_out, loss_target, m_meta_tokens, m_norm_w, m_w_in, m_sb_q_norm, m_sb_k_norm, m_gdn_conv_w, m_gdn_a_log, m_gdn_dt_bias, m_gdn_out_norm, m_hgrn_lb_logits, m_hgrn_out_norm, m_w_branch, m_w_out, v_meta_tokens, v_norm_w, v_w_in, v_sb_q_norm, v_sb_k_norm, v_gdn_conv_w, v_gdn_a_log, v_gdn_dt_bias, v_gdn_out_norm, v_hgrn_lb_logits, v_hgrn_out_norm, v_w_branch, v_w_out):
    depth = norm_w.shape[0]
    small_w = dict(norm_w=norm_w, sb_q_norm=sb_q_norm, sb_k_norm=sb_k_norm, gdn_a_log=gdn_a_log,
                   gdn_dt_bias=gdn_dt_bias, gdn_out_norm=gdn_out_norm, hgrn_lb_logits=hgrn_lb_logits,
                   hgrn_out_norm=hgrn_out_norm)
    small_m = dict(zip(_SMALL, (m_norm_w, m_sb_q_norm, m_sb_k_norm, m_gdn_a_log, m_gdn_dt_bias, m_gdn_out_norm,
                                m_hgrn_lb_logits, m_hgrn_out_norm)))
    small_v = dict(zip(_SMALL, (v_norm_w, v_sb_q_norm, v_sb_k_norm, v_gdn_a_log, v_gdn_dt_bias, v_gdn_out_norm,
                                v_hgrn_lb_logits, v_hgrn_out_norm)))

    w_in_g, w_br_g, w_out_g, conv_g, meta_g = gather_over_chips(
        [w_in.astype(BF16), w_branch.astype(BF16), w_out.astype(BF16), gdn_conv_w, meta_tokens])
    w_in_full = jnp.transpose(w_in_g, (1, 2, 0, 3)).reshape(depth, D_MODEL, N_IN)
    w_br_full = jnp.transpose(w_br_g, (1, 2, 3, 0, 4)).reshape(depth, N_BRANCHES, BRANCH, D_MODEL)
    w_out_full = jnp.transpose(w_out_g, (1, 0, 2, 3)).reshape(depth, D_MODEL, D_MODEL)
    conv_full = jnp.transpose(conv_g, (1, 2, 0, 3)).reshape(depth, 4, 3 * BRANCH)
    meta_full = jnp.transpose(meta_g, (1, 0, 2)).reshape(N_META, D_MODEL)

    sq_err, grad_x, d_meta, g = device_step(
        x[0], loss_target[0], meta_full, norm_w, w_in_full, sb_q_norm, sb_k_norm, conv_full, gdn_a_log, gdn_dt_bias,
        gdn_out_norm, hgrn_lb_logits, hgrn_out_norm, w_br_full, w_out_full)

    q4 = D_MODEL // 4
    n4 = N_IN // 4
    parts = [
        jnp.transpose(g["w_in"].reshape(depth, D_MODEL, 4, n4), (2, 0, 1, 3)).reshape(4, depth * D_MODEL, n4),
        jnp.transpose(g["w_branch"].reshape(depth, N_BRANCHES, BRANCH, 4, q4), (3, 0, 1, 2, 4)).reshape(4, -1, q4),
        jnp.transpose(g["w_out"].reshape(depth, 4, q4, D_MODEL), (1, 0, 2, 3)).reshape(4, depth * q4, D_MODEL),
        jnp.transpose(g["conv_w"].reshape(depth, 4, 4, 3 * BRANCH // 4), (2, 0, 1, 3)).reshape(4, depth * 4, -1),
        jnp.transpose(d_meta.reshape(N_META, 4, q4), (1, 0, 2)),
    ]
    received = scatter_over_chips([p.astype(BF16) for p in parts])
    core_sums = [sum_slabs(r) for r in received]
    sibling_sums = swap_with_sibling(core_sums)
    c = lax.axis_index("c")
    sharded = [(w_in, m_w_in, v_w_in), (w_branch, m_w_branch, v_w_branch), (w_out, m_w_out, v_w_out),
               (gdn_conv_w, m_gdn_conv_w, v_gdn_conv_w), (meta_tokens, m_meta_tokens, v_meta_tokens)]
    big = []
    for a, (w_, m_, v_) in enumerate(sharded):
        mine, other = core_sums[a], sibling_sums[a]
        first = jnp.where(c == 0, mine, other)
        second = jnp.where(c == 0, other, mine)
        shp = mine.shape
        outs = adamw(f"adamw_{a}", [first, second], w_.reshape(shp), m_.reshape(shp), v_.reshape(shp))
        big.append([o.reshape(w_.shape) for o in outs])

    small_shapes = [small_w[n].shape for n in _SMALL]
    gs = dict(norm_w=g["norm_w"], sb_q_norm=g["sb_qn"], sb_k_norm=g["sb_kn"], gdn_a_log=g["a_log"],
              gdn_dt_bias=g["dt_bias"], gdn_out_norm=g["gd_on"], hgrn_lb_logits=g["lb_logits"],
              hgrn_out_norm=g["hg_on"])
    n_lead = 128
    lead = [jnp.pad(sq_err.reshape(1), (0, n_lead - 1))]
    packed_g = gather_over_devices(_pack([gs[n] for n in _SMALL], lead))
    zeros_lead = [jnp.zeros((n_lead,), F32)]
    pw, pm, pv = (_pack([d[n] for n in _SMALL], zeros_lead) for d in (small_w, small_m, small_v))
    sg, sd, sm, sv_ = adamw("adamw_small", [packed_g[j] for j in range(8)], pw, pm, pv)
    loss = (0.5 / D_MODEL) * sg[0, 0]
    small = [dict(zip(_SMALL, _unpack(t, small_shapes, n_lead))) for t in (sg, sd, sm, sv_)]

    order = ("meta_tokens", "norm_w", "w_in", "sb_q_norm", "sb_k_norm", "gdn_conv_w", "gdn_a_log", "gdn_dt_bias",
             "gdn_out_norm", "hgrn_lb_logits", "hgrn_out_norm", "w_branch", "w_out")
    big_idx = dict(w_in=0, w_branch=1, w_out=2, gdn_conv_w=3, meta_tokens=4)
    result = [loss, grad_x[None]]
    for kind in range(4):
        for n in order:
            result.append(big[big_idx[n]][kind] if n in big_idx else small[kind][n])
    return tuple(result)
```

```python
import functools
import math

import jax
import jax.numpy as jnp
from jax import lax
from jax.experimental import pallas as pl
from jax.experimental.pallas import tpu as pltpu

F32, BF16 = jnp.float32, jnp.bfloat16
MESH = pl.DeviceIdType.MESH

D_MODEL = 1024
BRANCH = 512
HEAD = 128
N_HEADS = 4
N_BRANCHES = 3
CHUNK = 64
N_META = 16
FRONT = 128
PAD_FRONT = FRONT - N_META
EPS = 1e-6
SB_KEY_BLOCK = 640
SB_LANE_GROUP = 128
HALO = 8
ROW_TILE = 320
SB_Q_TILE = 320
GDN_TILE = 320
HGRN_SUB = 16
LOSS_TILE = 128
VMEM_LIMIT = 56 * 1024 * 1024

ADAM_LR, ADAM_B1, ADAM_B2, ADAM_EPS, ADAM_WD, ADAM_STEP = 0.001, 0.9, 0.999, 1e-08, 0.01, 10

_C_SB_QKV = (0, 1536)
_C_SB_Z = (1536, 2048)
_C_GD_QKV = (2048, 3584)
_C_GD_Z = (3584, 4096)
_C_GD_BA = (4096, 4104)
_C_HG_QFI = (4104, 5640)
_C_HG_Z = (5640, 6152)
_C_MIX = (6152, 9224)
N_IN = 9224


def _mxu(a):
    return a.astype(BF16)


def _split3(x):
    x1 = _mxu(x)
    r = x - x1.astype(F32)
    x2 = _mxu(r)
    return x1, x2, _mxu(r - x2.astype(F32))


def _stack6(a, b, ca, cb):
    a1, a2, a3 = _split3(a)
    b1, b2, b3 = _split3(b)
    return jnp.concatenate([a1, a1, a2, a1, a2, a3], axis=ca), jnp.concatenate([b1, b2, b1, b3, b2, b1], axis=cb)


def _dg(a, b, ca, cb, hi=False):
    dn = (((ca,), (cb,)), ((), ()))
    if hi:
        a, b = _stack6(a, b, ca, cb)
    return lax.dot_general(_mxu(a), _mxu(b), dn, preferred_element_type=F32)


def _make_mm(hi):
    @jax.custom_vjp
    def nn(a, b):
        return _dg(a, b, 1, 0, hi)

    @jax.custom_vjp
    def nt(a, b):
        return _dg(a, b, 1, 1, hi)

    @jax.custom_vjp
    def tn(a, b):
        return _dg(a, b, 0, 0, hi)

    nn.defvjp(lambda a, b: (nn(a, b), (a, b)), lambda r, g: (nt(g, r[1]), tn(r[0], g)))
    nt.defvjp(lambda a, b: (nt(a, b), (a, b)), lambda r, g: (nn(g, r[1]), tn(g, r[0])))
    tn.defvjp(lambda a, b: (tn(a, b), (a, b)), lambda r, g: (nt(r[1], g), nn(r[0], g)))
    return nn, nt, tn


mm_nn, mm_nt, mm_tn = _make_mm(False)
mh_nn, mh_nt, mh_tn = _make_mm(True)


@jax.custom_vjp
def _mm_w(a, w, wz):
    return _dg(a, w, 1, 0)


def _mm_w_fwd(a, w, wz):
    return _dg(a, w, 1, 0), (a, w)


def _mm_w_bwd(res, g):
    a, w = res
    return _dg(g, w, 1, 1), jnp.zeros_like(w), _dg(a, g, 0, 0)


_mm_w.defvjp(_mm_w_fwd, _mm_w_bwd)


def mmw(a, w, wz):
    return _dg(a, w, 1, 0) if wz is None else _mm_w(a, w, wz)


@functools.partial(jax.custom_vjp, nondiff_argnums=(1,))
def roll_rows(x, s):
    return pltpu.roll(x, s, 0)


roll_rows.defvjp(lambda x, s: (pltpu.roll(x, s, 0), None),
                 lambda s, r, g: (pltpu.roll(g, g.shape[0] - s, 0),))


def _sigmoid(x):
    return 0.5 * (jnp.tanh(0.5 * x) + 1.0)


def _silu(x):
    return x * _sigmoid(x)


def _softplus(x):
    return jnp.maximum(x, 0.0) + jnp.log(1.0 + jnp.exp(-jnp.abs(x)))


def _rms(x, w):
    return x * lax.rsqrt(jnp.mean(x * x, axis=-1, keepdims=True) + EPS) * w


def _l2(x):
    return x * lax.rsqrt(jnp.sum(x * x, axis=-1, keepdims=True) + EPS)


def _heads(x, fn):
    return jnp.concatenate([fn(x[:, h * HEAD:(h + 1) * HEAD]) for h in range(N_HEADS)], axis=-1)


def _valid_rows(row0, n):
    rows = row0 + lax.broadcasted_iota(jnp.int32, (n, 1), 0)
    return (rows >= PAD_FRONT).astype(F32)


def _wz(wzs, i):
    return None if wzs is None else wzs[i]


def f_sb_pre(vals, p, w, wzs, row0):
    (h,), (norm_w, qn, kn) = vals, p
    raw = mmw(_rms(h, norm_w), w[0], _wz(wzs, 0))
    q = _heads(raw[:, :BRANCH], lambda t: _rms(t, qn))
    k = _heads(raw[:, BRANCH:2 * BRANCH], lambda t: _rms(t, kn))
    return [q, k, raw[:, 2 * BRANCH:]]


def f_gd_pre(vals, p, w, wzs, row0):
    (h_ext,), (norm_w, conv_w, a_log, dt_bias) = vals, p
    tm = h_ext.shape[0] - HALO
    raw = mmw(_rms(h_ext, norm_w), w[0], _wz(wzs, 0))
    x = raw[:, :3 * BRANCH]
    y = conv_w[3:4] * x
    for i in range(3):
        y = y + conv_w[i:i + 1] * roll_rows(x, 3 - i)
    y = _silu(y[HALO:])
    gq = _heads(y[:, :BRANCH], _l2) * (HEAD ** -0.5)
    gk = _heads(y[:, BRANCH:2 * BRANCH], _l2)
    gv = y[:, 2 * BRANCH:]
    slab = raw[HALO:, 3 * BRANCH:]
    lane = lax.broadcasted_iota(jnp.int32, (1, HEAD), 1)
    beta = _sigmoid(slab) * _valid_rows(row0, tm)
    g = -jnp.exp(a_log) * _softplus(slab + dt_bias)
    bg = jnp.where(lane < N_HEADS, beta, jnp.where(lane < 2 * N_HEADS, g, 0.0))
    return [gq, gk, gv, bg]


def f_hg_pre(vals, p, w, wzs, row0):
    (h,), (norm_w, lb) = vals, p
    raw = mmw(_rms(h, norm_w), w[0], _wz(wzs, 0))
    hq = _silu(raw[:, :BRANCH])
    fp = raw[:, BRANCH:2 * BRANCH]
    forget = lb + (1.0 - lb) * _sigmoid(fp)
    hk = (1.0 - lb) * _sigmoid(-fp)
    hv = raw[:, 2 * BRANCH:] * _valid_rows(row0, h.shape[0])
    return [hq, hk, hv, jnp.log(forget)]


def _merge_u(h, o, norm_w, out_norm, w, wzs, normed):
    xn = _rms(h, norm_w)
    z = mmw(xn, w[0], _wz(wzs, 0))
    mix = mmw(xn, w[1], _wz(wzs, 1))
    if normed:
        o = _heads(o, lambda t: _rms(t, out_norm))
    return _sigmoid(mix) * mmw(o * _silu(z), w[2], _wz(wzs, 2))


def make_f_merge(normed, with_prev):
    def f(vals, p, w, wzs, row0):
        u = _merge_u(vals[0], vals[1], p[0], p[1], w, wzs, normed)
        return [vals[2] + u] if with_prev else [u]
    return f


def f_out(vals, p, w, wzs, row0):
    h, y = vals
    return [h + mmw(y, w[0], _wz(wzs, 0))]


def _full_spec(a):
    nd = a.ndim
    return pl.BlockSpec(a.shape, lambda i, _nd=nd: (0,) * _nd)


def stage_fwd(name, fn, rows, params, weights, out_widths, *, halo=False, tm=ROW_TILE):
    t_len = rows[0].shape[0]
    n = t_len // tm
    nr, npar, nw = len(rows), len(params), len(weights)

    def body(*refs):
        i = pl.program_id(0)
        refs = list(refs)
        prev_ref = refs.pop(0) if halo else None
        row_refs, refs = refs[:nr], refs[nr:]
        par_refs, refs = refs[:npar], refs[npar:]
        w_refs, out_refs = refs[:nw], refs[nw:]
        vals = [r[...] for r in row_refs]
        if halo:
            prev = jnp.where(i == 0, 0.0, prev_ref[...])
            vals[0] = jnp.concatenate([prev, vals[0]], axis=0)
        outs = fn(vals, [r[...] for r in par_refs], [r[...] for r in w_refs], None, i * tm)
        for o_ref, o in zip(out_refs, outs):
            o_ref[...] = o

    in_specs, args = [], []
    if halo:
        in_specs.append(pl.BlockSpec((HALO, rows[0].shape[1]),
                                     lambda i: (jnp.maximum(i * (tm // HALO) - 1, 0), 0)))
        args.append(rows[0])
    for r in rows:
        in_specs.append(pl.BlockSpec((tm, r.shape[1]), lambda i: (i, 0)))
        args.append(r)
    for a in list(params) + list(weights):
        in_specs.append(_full_spec(a))
        args.append(a)
    return pl.pallas_call(
        body, grid=(n,), in_specs=in_specs,
        out_specs=[pl.BlockSpec((tm, wd), lambda i: (i, 0)) for wd in out_widths],
        out_shape=[jax.ShapeDtypeStruct((t_len, wd), F32) for wd in out_widths],
        compiler_params=pltpu.CompilerParams(dimension_semantics=("arbitrary",), vmem_limit_bytes=VMEM_LIMIT),
        name=name,
    )(*args)


def stage_vjp(name, fn, rows, params, weights, cts, diff_rows, *, halo=False, acc_in=None, tm=ROW_TILE):
    t_len = rows[0].shape[0]
    n = t_len // tm
    nr, npar, nw, nct, nd = len(rows), len(params), len(weights), len(cts), len(diff_rows)
    has_acc = acc_in is not None

    def body(*refs):
        i = pl.program_id(0)
        tile = n - 1 - i
        refs = list(refs)
        prev_ref = refs.pop(0) if halo else None
        row_refs, refs = refs[:nr], refs[nr:]
        par_refs, refs = refs[:npar], refs[npar:]
        w_refs, refs = refs[:nw], refs[nw:]
        ct_refs, refs = refs[:nct], refs[nct:]
        acc_ref = refs.pop(0) if has_acc else None
        drow_refs, refs = refs[:nd], refs[nd:]
        dpar_refs, refs = refs[:npar], refs[npar:]
        dw_refs, refs = refs[:nw], refs[nw:]
        carry_ref = refs[0] if halo else None

        vals = [r[...] for r in row_refs]
        if halo:
            prev = jnp.where(tile == 0, 0.0, prev_ref[...])
            vals[0] = jnp.concatenate([prev, vals[0]], axis=0)
        pvals = [r[...] for r in par_refs]
        wvals = [r[...] for r in w_refs]
        wzs = [jnp.zeros(w.shape, F32) for w in wvals]

        def f(dvals, pv, wz):
            full = list(vals)
            for k, idx in enumerate(diff_rows):
                full[idx] = dvals[k]
            return fn(full, pv, wvals, wz, tile * tm)

        _, vjp = jax.vjp(f, [vals[k] for k in diff_rows], pvals, wzs)
        d_rows, d_par, d_w = vjp([r[...] for r in ct_refs])

        @pl.when(i == 0)
        def _():
            for r in list(dpar_refs) + list(dw_refs):
                r[...] = jnp.zeros(r.shape, F32)

        for r, g in zip(list(dpar_refs) + list(dw_refs), list(d_par) + list(d_w)):
            r[...] += g
        for k, (r, g) in enumerate(zip(drow_refs, d_rows)):
            if k == 0 and diff_rows[0] == 0:
                if halo:
                    g_ext = g
                    g = g_ext[HALO:]
                    tail = g[tm - HALO:] + jnp.where(i == 0, 0.0, carry_ref[...])
                    g = jnp.concatenate([g[:tm - HALO], tail], axis=0)
                    carry_ref[...] = g_ext[:HALO]
                if has_acc:
                    g = g + acc_ref[...]
            r[...] = g

    rev = lambda i: (n - 1 - i, 0)
    in_specs, args = [], []
    if halo:
        in_specs.append(pl.BlockSpec((HALO, rows[0].shape[1]),
                                     lambda i: (jnp.maximum((n - 1 - i) * (tm // HALO) - 1, 0), 0)))
        args.append(rows[0])
    for r in list(rows):
        in_specs.append(pl.BlockSpec((tm, r.shape[1]), rev))
        args.append(r)
    for a in list(params) + list(weights):
        in_specs.append(_full_spec(a))
        args.append(a)
    for c in cts:
        in_specs.append(pl.BlockSpec((tm, c.shape[1]), rev))
        args.append(c)
    if has_acc:
        in_specs.append(pl.BlockSpec((tm, acc_in.shape[1]), rev))
        args.append(acc_in)
    out_specs = [pl.BlockSpec((tm, rows[k].shape[1]), rev) for k in diff_rows]
    out_shape = [jax.ShapeDtypeStruct(rows[k].shape, F32) for k in diff_rows]
    for a in list(params) + list(weights):
        out_specs.append(_full_spec(a))
        out_shape.append(jax.ShapeDtypeStruct(a.shape, F32))
    scratch = [pltpu.VMEM((HALO, rows[0].shape[1]), F32)] if halo else []
    outs = pl.pallas_call(
        body, grid=(n,), in_specs=in_specs, out_specs=out_specs, out_shape=out_shape, scratch_shapes=scratch,
        compiler_params=pltpu.CompilerParams(dimension_semantics=("arbitrary",), vmem_limit_bytes=VMEM_LIMIT),
        name=name,
    )(*args)
    return outs[:nd], outs[nd:nd + npar], outs[nd + npar:]


def _sb_tile(qb, kj, t_idx, ks):
    z = _dg(qb, kj, 1, 1)
    sp = jnp.log(1.0 + jnp.exp(-jnp.abs(z)))
    ls = jnp.minimum(z, 0.0) - sp
    if t_idx is None:
        return None, ls, ls - z
    s_idx = ks + lax.broadcasted_iota(jnp.int32, (1, SB_KEY_BLOCK), 1)
    mask = (s_idx < t_idx) & (s_idx >= PAD_FRONT)
    lk = jnp.where(mask, ls - z, 0.0)
    return mask, ls, lk


def _masked(mask, x):
    return x if mask is None else jnp.where(mask, x, 0.0)


def _walk_key_blocks(n_blocks, step, carry):
    carry = step(0, carry, True)
    last = jnp.maximum(n_blocks - 1, 1)
    carry = lax.fori_loop(1, last, lambda jj, c: step(jj, c, False), carry)
    return lax.fori_loop(last, n_blocks, lambda jj, c: step(jj, c, True), carry)


def _tri(n, kind):
    r = lax.broadcasted_iota(jnp.int32, (n, n), 0)
    c = lax.broadcasted_iota(jnp.int32, (n, n), 1)
    return {"gt": r > c, "ge": r >= c, "eq": r == c}[kind]


def _suffix_sums(x, tri, carry, inclusive):
    groups = []
    for g in reversed(range(x.shape[1] // SB_LANE_GROUP)):
        xg = x[:, g * SB_LANE_GROUP:(g + 1) * SB_LANE_GROUP]
        hi = _mxu(xg)
        r1 = xg - hi.astype(F32)
        mid = _mxu(r1)
        terms = [hi, mid] if tri.shape[0] == 2 * SB_LANE_GROUP else [hi, mid, _mxu(r1 - mid.astype(F32))]
        sg = _dg(jnp.concatenate(terms, axis=1), tri, 1, 0)
        groups.append(sg + carry)
        carry = carry + (sg[:, 0:1] if inclusive else sg[:, 0:1] + xg[:, 0:1])
    return jnp.concatenate(groups[::-1], axis=1), carry


def sb_fwd(q, k, v, *, tq=SB_Q_TILE):
    t_len = q.shape[0]
    nq = t_len // tq
    kb = SB_KEY_BLOCK

    def body(q_ref, k_ref, v_ref, o_ref):
        i = pl.program_id(1)
        qb = _mxu(q_ref[...] * (HEAD ** -0.5))
        t_idx = i * tq + lax.broadcasted_iota(jnp.int32, (tq, 1), 0)
        after = _mxu(jnp.tile(_tri(SB_LANE_GROUP, "gt").astype(F32), (2, 1)))
        o_ref[...] = jnp.zeros(o_ref.shape, F32)
        n_blocks = ((i + 1) * tq + kb - 1) // kb

        def step(jj, run, masked):
            ks = pl.multiple_of((n_blocks - 1 - jj) * kb, kb)
            kj = _mxu(k_ref[pl.ds(ks, kb), :])
            vj = _mxu(v_ref[pl.ds(ks, kb), :])
            mask, ls, lk = _sb_tile(qb, kj, t_idx if masked else None, ks)
            passed, run = _suffix_sums(lk, after, run, False)
            a = _masked(mask, jnp.exp(ls + passed))
            o_ref[...] += _dg(a, vj, 1, 0)
            return run

        _walk_key_blocks(n_blocks, step, jnp.zeros((tq, 1), F32))

    col = pl.BlockSpec((t_len, HEAD), lambda h, i: (0, h))
    tile = pl.BlockSpec((tq, HEAD), lambda h, i: (i, h))
    return pl.pallas_call(
        body, grid=(N_HEADS, nq), in_specs=[tile, col, col], out_specs=tile,
        out_shape=jax.ShapeDtypeStruct((t_len, BRANCH), F32),
        compiler_params=pltpu.CompilerParams(dimension_semantics=("arbitrary", "arbitrary"),
                                             vmem_limit_bytes=VMEM_LIMIT),
        name="sb_fwd",
    )(q, k, v)


def sb_bwd(q, k, v, o, do, *, tq=SB_Q_TILE):
    t_len = q.shape[0]
    nq = t_len // tq
    kb = SB_KEY_BLOCK
    scale = HEAD ** -0.5

    def body(q_ref, k_ref, v_ref, o_ref, do_ref, dq_ref, dk_ref, dv_ref):
        i = pl.program_id(1)

        @pl.when(i == 0)
        def _():
            dk_ref[...] = jnp.zeros(dk_ref.shape, F32)
            dv_ref[...] = jnp.zeros(dv_ref.shape, F32)

        qb = _mxu(q_ref[...] * scale)
        do_f = do_ref[...]
        dob = _mxu(do_f)
        total = jnp.sum(dob.astype(F32) * o_ref[...], axis=-1, keepdims=True)
        t_idx = i * tq + lax.broadcasted_iota(jnp.int32, (tq, 1), 0)
        after = _mxu(jnp.tile(_tri(SB_LANE_GROUP, "gt").astype(F32), (2, 1)))
        from_here = _mxu(jnp.tile(_tri(SB_LANE_GROUP, "ge").astype(F32), (3, 1)))
        dq_ref[...] = jnp.zeros(dq_ref.shape, F32)
        n_blocks = ((i + 1) * tq + kb - 1) // kb

        def step(jj, carry, masked):
            run, run_dl = carry
            ks = pl.multiple_of((n_blocks - 1 - jj) * kb, kb)
            kj = _mxu(k_ref[pl.ds(ks, kb), :])
            vj = _mxu(v_ref[pl.ds(ks, kb), :])
            mask, ls, lk = _sb_tile(qb, kj, t_idx if masked else None, ks)
            passed, run = _suffix_sums(lk, after, run, False)
            a = _masked(mask, jnp.exp(ls + passed))
            ab = _mxu(a)
            dl = ab.astype(F32) * _dg(dob, vj, 1, 1)
            from_s, run_dl = _suffix_sums(dl, from_here, run_dl, True)
            before = total - from_s
            sig = jnp.exp(ls)
            dzb = _mxu(dl * (1.0 - sig) - _masked(mask, sig * before))
            dq_ref[...] += _dg(dzb, kj, 1, 0)
            dk_ref[pl.ds(ks, kb), :] += _dg(dzb, qb, 0, 0)
            dv_ref[pl.ds(ks, kb), :] += _dg(ab, dob, 0, 0)
            return run, run_dl

        zero = jnp.zeros((tq, 1), F32)
        _walk_key_blocks(n_blocks, step, (zero, zero))
        dq_ref[...] *= scale

    col = pl.BlockSpec((t_len, HEAD), lambda h, i: (0, h))
    tile = pl.BlockSpec((tq, HEAD), lambda h, i: (i, h))
    full = jax.ShapeDtypeStruct((t_len, BRANCH), F32)
    return pl.pallas_call(
        body, grid=(N_HEADS, nq), in_specs=[tile, col, col, tile, tile], out_specs=[tile, col, col],
        out_shape=[full, full, full],
        compiler_params=pltpu.CompilerParams(dimension_semantics=("arbitrary", "arbitrary"),
                                             vmem_limit_bytes=VMEM_LIMIT),
        name="sb_bwd",
    )(q, k, v, o, do)


def _bdg(a, b, ca, cb, hi=False):
    dn = (((ca,), (cb,)), ((0,), (0,)))
    if hi:
        a, b = _stack6(a, b, ca, cb)
    return lax.dot_general(_mxu(a), _mxu(b), dn, preferred_element_type=F32)


def _make_bmm(hi):
    @jax.custom_vjp
    def nn(a, b):
        return _bdg(a, b, 2, 1, hi)

    @jax.custom_vjp
    def nt(a, b):
        return _bdg(a, b, 2, 2, hi)

    @jax.custom_vjp
    def tn(a, b):
        return _bdg(a, b, 1, 1, hi)

    nn.defvjp(lambda a, b: (nn(a, b), (a, b)), lambda r, g: (nt(g, r[1]), tn(r[0], g)))
    nt.defvjp(lambda a, b: (nt(a, b), (a, b)), lambda r, g: (nn(g, r[1]), tn(g, r[0])))
    tn.defvjp(lambda a, b: (tn(a, b), (a, b)), lambda r, g: (nt(r[1], g), nn(r[0], g)))
    return nn, nt, tn


bm_nn, bm_nt, bm_tn = _make_bmm(False)
bh_nn, bh_nt, bh_tn = _make_bmm(True)


def gdn_tile(state, q, k, v, bg):
    c = CHUNK
    nc = q.shape[0] // c
    pairs = [(ci, h) for ci in range(nc) for h in range(N_HEADS)]
    nb = len(pairs)
    rows = lambda ci: slice(ci * c, (ci + 1) * c)
    split = lambda x: jnp.stack([x[rows(ci), h * HEAD:(h + 1) * HEAD] for ci, h in pairs])
    qs, ks, vs = split(q), split(k), split(v)
    beta = jnp.stack([bg[rows(ci), h:h + 1] for ci, h in pairs])
    g = jnp.stack([bg[rows(ci), N_HEADS + h:N_HEADS + h + 1] for ci, h in pairs])
    causal = jnp.broadcast_to(_tri(c, "ge")[None], (nb, c, c))
    strict = jnp.broadcast_to(_tri(c, "gt")[None], (nb, c, c))
    eye = jnp.broadcast_to(_tri(c, "eq")[None], (nb, c, c)).astype(F32)
    g_lanes = bh_nn(causal.astype(F32), g * jnp.ones((1, 1, HEAD), F32))
    g_col = g_lanes[:, :, :c]
    g_row = bh_nn(jnp.ones((nb, c, c), F32), eye * g_col)
    decay = jnp.where(causal, jnp.exp(jnp.where(causal, g_col - g_row, 0.0)), 0.0)
    kb = ks * beta
    m = jnp.where(strict, bm_nt(kb, ks) * decay, 0.0)
    inv = eye - m
    p = bh_nn(m, m)
    for level in range(5):
        inv = inv + bh_nn(inv, p)
        if level < 4:
            p = bh_nn(p, p)
    g1 = g_lanes[:, :, 0:1]
    g_last = g1[:, c - 1:c, :]
    u = bh_nn(inv, vs * beta)
    w = bh_nn(inv, kb * jnp.exp(g1))
    a_qk = jnp.where(causal, bm_nt(qs, ks) * decay, 0.0)
    qd = qs * jnp.exp(g1)
    kd = ks * jnp.exp(g_last - g1)
    gl = jnp.exp(g_last)
    outs = []
    for ci in range(nc):
        sl = slice(ci * N_HEADS, (ci + 1) * N_HEADS)
        v_new = u[sl] - bm_nn(w[sl], state)
        o = bm_nn(qd[sl], state) + bm_nn(a_qk[sl], v_new)
        state = state * gl[sl] + bm_tn(kd[sl], v_new)
        outs.append(jnp.concatenate([o[h] for h in range(N_HEADS)], axis=1))
    return state, jnp.concatenate(outs, axis=0)


def gdn_fwd(q, k, v, bg, *, tg=GDN_TILE):
    t_len = q.shape[0]
    n = t_len // tg

    def body(q_ref, k_ref, v_ref, bg_ref, o_ref, s_all_ref, s_ref):
        @pl.when(pl.program_id(0) == 0)
        def _():
            s_ref[...] = jnp.zeros(s_ref.shape, F32)

        s_in = s_ref[...]
        s_all_ref[0] = s_in
        s_out, o = gdn_tile(s_in, q_ref[...], k_ref[...], v_ref[...], bg_ref[...])
        o_ref[...] = o
        s_ref[...] = s_out

    row = lambda wd: pl.BlockSpec((tg, wd), lambda i: (i, 0))
    return pl.pallas_call(
        body, grid=(n,), in_specs=[row(BRANCH), row(BRANCH), row(BRANCH), row(HEAD)],
        out_specs=[row(BRANCH), pl.BlockSpec((1, N_HEADS, HEAD, HEAD), lambda i: (i, 0, 0, 0))],
        out_shape=[jax.ShapeDtypeStruct((t_len, BRANCH), F32),
                   jax.ShapeDtypeStruct((n, N_HEADS, HEAD, HEAD), F32)],
        scratch_shapes=[pltpu.VMEM((N_HEADS, HEAD, HEAD), F32)],
        compiler_params=pltpu.CompilerParams(dimension_semantics=("arbitrary",), vmem_limit_bytes=VMEM_LIMIT),
        name="gdn_fwd",
    )(q, k, v, bg)


def gdn_bwd(q, k, v, bg, s_all, do, *, tg=GDN_TILE):
    t_len = q.shape[0]
    n = t_len // tg

    def body(q_ref, k_ref, v_ref, bg_ref, s_all_ref, do_ref, dq_ref, dk_ref, dv_ref, dbg_ref, ds_ref):
        @pl.when(pl.program_id(0) == 0)
        def _():
            ds_ref[...] = jnp.zeros(ds_ref.shape, F32)

        _, vjp = jax.vjp(gdn_tile, s_all_ref[0], q_ref[...], k_ref[...], v_ref[...], bg_ref[...])
        ds, dq, dk, dv, dbg = vjp((ds_ref[...], do_ref[...]))
        ds_ref[...] = ds
        dq_ref[...] = dq
        dk_ref[...] = dk
        dv_ref[...] = dv
        dbg_ref[...] = dbg

    row = lambda wd: pl.BlockSpec((tg, wd), lambda i: (n - 1 - i, 0))
    wide, slab = jax.ShapeDtypeStruct((t_len, BRANCH), F32), jax.ShapeDtypeStruct((t_len, HEAD), F32)
    return pl.pallas_call(
        body, grid=(n,),
        in_specs=[row(BRANCH), row(BRANCH), row(BRANCH), row(HEAD),
                  pl.BlockSpec((1, N_HEADS, HEAD, HEAD), lambda i: (n - 1 - i, 0, 0, 0)), row(BRANCH)],
        out_specs=[row(BRANCH), row(BRANCH), row(BRANCH), row(HEAD)],
        out_shape=[wide, wide, wide, slab],
        scratch_shapes=[pltpu.VMEM((N_HEADS, HEAD, HEAD), F32)],
        compiler_params=pltpu.CompilerParams(dimension_semantics=("arbitrary",), vmem_limit_bytes=VMEM_LIMIT),
        name="gdn_bwd",
    )(q, k, v, bg, s_all, do)


def hgrn_chunk(state_t, q, k, v, g):
    c, sub = CHUNK, HGRN_SUB
    nsub = c // sub
    gc = mh_nn(_tri(c, "ge").astype(F32), g)
    o = mm_nt(q * jnp.exp(gc), state_t)
    blk = lambda x, b: x[b * sub:(b + 1) * sub]
    anchors = [gc[b * sub:b * sub + 1] for b in range(nsub)]
    q_anch = jnp.stack([blk(q, b) * jnp.exp(blk(gc, b) - anchors[b]) for b in range(nsub)])
    k_anch = jnp.stack([k * jnp.exp(jnp.minimum(anchors[b] - gc, 0.0)) for b in range(nsub)])
    a_far = bm_nt(q_anch, k_anch)
    a_far = jnp.concatenate([a_far[b] for b in range(nsub)], axis=0)
    r_i = lax.broadcasted_iota(jnp.int32, (c, c), 0)
    c_i = lax.broadcasted_iota(jnp.int32, (c, c), 1)
    o = o + mm_nn(jnp.where(c_i < jnp.bitwise_and(r_i, -sub), a_far, 0.0), v)
    row = lax.broadcasted_iota(jnp.int32, (c, 1), 0)
    for d in range(sub):
        kr, gr, vr = (k, gc, v) if d == 0 else (roll_rows(k, d), roll_rows(gc, d), roll_rows(v, d))
        ok = jnp.bitwise_and(row, sub - 1) >= d
        e = jnp.exp(jnp.where(ok, gc - gr, 0.0))
        a = jnp.where(ok, jnp.sum(q * kr * e, axis=-1, keepdims=True), 0.0)
        o = o + a * vr
    g_end = gc[c - 1:c, :]
    state_t = state_t * jnp.exp(g_end) + mm_tn(v, k * jnp.exp(g_end - gc))
    return state_t, o


def hgrn_fwd(q, k, v, g, *, tg=ROW_TILE):
    t_len = q.shape[0]
    n, nc = t_len // tg, tg // CHUNK

    def body(q_ref, k_ref, v_ref, g_ref, o_ref, s_all_ref, s_ref):
        @pl.when(pl.program_id(1) == 0)
        def _():
            s_ref[...] = jnp.zeros(s_ref.shape, F32)

        def step(c, _):
            r = pl.ds(pl.multiple_of(c * CHUNK, CHUNK), CHUNK)
            s_in = s_ref[...]
            s_all_ref[c, 0] = s_in
            s_out, o = hgrn_chunk(s_in, q_ref[r, :], k_ref[r, :], v_ref[r, :], g_ref[r, :])
            o_ref[r, :] = o
            s_ref[...] = s_out
            return 0

        lax.fori_loop(0, nc, step, 0)

    row = pl.BlockSpec((tg, HEAD), lambda h, i: (i, h))
    return pl.pallas_call(
        body, grid=(N_HEADS, n), in_specs=[row, row, row, row],
        out_specs=[row, pl.BlockSpec((nc, 1, HEAD, HEAD), lambda h, i: (i, h, 0, 0))],
        out_shape=[jax.ShapeDtypeStruct((t_len, BRANCH), F32),
                   jax.ShapeDtypeStruct((t_len // CHUNK, N_HEADS, HEAD, HEAD), F32)],
        scratch_shapes=[pltpu.VMEM((HEAD, HEAD), F32)],
        compiler_params=pltpu.CompilerParams(dimension_semantics=("arbitrary", "arbitrary"),
                                             vmem_limit_bytes=VMEM_LIMIT),
        name="hgrn_fwd",
    )(q, k, v, g)


def hgrn_bwd(q, k, v, g, s_all, do, *, tg=ROW_TILE):
    t_len = q.shape[0]
    n, nc = t_len // tg, tg // CHUNK

    def body(q_ref, k_ref, v_ref, g_ref, s_all_ref, do_ref, dq_ref, dk_ref, dv_ref, dg_ref, ds_ref):
        @pl.when(pl.program_id(1) == 0)
        def _():
            ds_ref[...] = jnp.zeros(ds_ref.shape, F32)

        def step(cc, _):
            c = nc - 1 - cc
            r = pl.ds(pl.multiple_of(c * CHUNK, CHUNK), CHUNK)
            _, vjp = jax.vjp(hgrn_chunk, s_all_ref[c, 0], q_ref[r, :], k_ref[r, :], v_ref[r, :], g_ref[r, :])
            ds, dq, dk, dv, dg = vjp((ds_ref[...], do_ref[r, :]))
            ds_ref[...] = ds
            dq_ref[r, :] = dq
            dk_ref[r, :] = dk
            dv_ref[r, :] = dv
            dg_ref[r, :] = dg
            return 0

        lax.fori_loop(0, nc, step, 0)

    row = pl.BlockSpec((tg, HEAD), lambda h, i: (n - 1 - i, h))
    wide = jax.ShapeDtypeStruct((t_len, BRANCH), F32)
    return pl.pallas_call(
        body, grid=(N_HEADS, n),
        in_specs=[row, row, row, row, pl.BlockSpec((nc, 1, HEAD, HEAD), lambda h, i: (n - 1 - i, h, 0, 0)), row],
        out_specs=[row, row, row, row], out_shape=[wide, wide, wide, wide],
        scratch_shapes=[pltpu.VMEM((HEAD, HEAD), F32)],
        compiler_params=pltpu.CompilerParams(dimension_semantics=("arbitrary", "arbitrary"),
                                             vmem_limit_bytes=VMEM_LIMIT),
        name="hgrn_bwd",
    )(q, k, v, g, s_all, do)


def loss_head(h, target, *, tl=LOSS_TILE):
    seq = target.shape[0]
    n = seq // tl
    off = FRONT // tl

    def body(h_ref, t_ref, sq_ref, dy_ref):
        @pl.when(pl.program_id(0) == 0)
        def _():
            sq_ref[...] = jnp.zeros(sq_ref.shape, F32)

        err = h_ref[...] - t_ref[...]
        sq_ref[...] += jnp.sum(err * err, keepdims=True)
        dy_ref[...] = err * (1.0 / D_MODEL)

    return pl.pallas_call(
        body, grid=(n,),
        in_specs=[pl.BlockSpec((tl, D_MODEL), lambda i: (i + off, 0)), pl.BlockSpec((tl, D_MODEL), lambda i: (i, 0))],
        out_specs=[pl.BlockSpec((1, 1), lambda i: (0, 0)), pl.BlockSpec((tl, D_MODEL), lambda i: (i, 0))],
        out_shape=[jax.ShapeDtypeStruct((1, 1), F32), jax.ShapeDtypeStruct((seq, D_MODEL), F32)],
        compiler_params=pltpu.CompilerParams(dimension_semantics=("arbitrary",)),
        name="loss_head",
    )(h, target)


def _pad_lanes(a, lo, n=HEAD):
    return jnp.pad(a.astype(F32), (lo, n - lo - a.shape[0])).reshape(1, n)


def _lower_bounds(logits):
    p = jax.nn.softmax(logits.astype(F32), axis=0)
    return jnp.cumsum(p, axis=0) - p[0:1]


def _layer_weights(w_in_l, w_branch_l, w_out_l):
    c = lambda r: w_in_l[:, r[0]:r[1]]
    ba = jnp.pad(c(_C_GD_BA), ((0, 0), (0, HEAD - 2 * N_HEADS)))
    mix = c(_C_MIX)
    return dict(
        sb=c(_C_SB_QKV), gd=jnp.concatenate([c(_C_GD_QKV), ba], axis=1), hg=c(_C_HG_QFI),
        z=[c(_C_SB_Z), c(_C_GD_Z), c(_C_HG_Z)],
        mix=[mix[:, b * D_MODEL:(b + 1) * D_MODEL] for b in range(N_BRANCHES)],
        br=[w_branch_l[b] for b in range(N_BRANCHES)], out=w_out_l)


def device_step(x, target, meta, norm_w, w_in, sb_qn, sb_kn, conv_w, a_log, dt_bias, gd_on, lb_logits, hg_on,
                w_branch, w_out):
    depth = norm_w.shape[0]
    lbs, lb_vjp = jax.vjp(_lower_bounds, lb_logits)
    h = jnp.concatenate([jnp.zeros((PAD_FRONT, D_MODEL), F32), meta, x], axis=0)
    row = lambda a: a.reshape(1, -1)
    saved = []
    for l in range(depth):
        w = _layer_weights(w_in[l], w_branch[l], w_out[l])
        nw = row(norm_w[l])
        p_sb = [nw, row(sb_qn[l]), row(sb_kn[l])]
        p_gd = [nw, conv_w[l], _pad_lanes(a_log[l], N_HEADS), _pad_lanes(dt_bias[l], N_HEADS)]
        p_hg = [nw, row(lbs[l])]
        out_norms = [row(sb_qn[l]), row(gd_on[l]), row(hg_on[l])]
        sq, sk, sv = stage_fwd("sb_pre", f_sb_pre, [h], p_sb, [w["sb"]], [BRANCH] * 3)
        o_sb = sb_fwd(sq, sk, sv)
        gq, gk, gv, bg = stage_fwd("gd_pre", f_gd_pre, [h], p_gd, [w["gd"]], [BRANCH] * 3 + [HEAD], halo=True)
        o_gd, gd_states = gdn_fwd(gq, gk, gv, bg)
        hq, hk, hv, hg = stage_fwd("hg_pre", f_hg_pre, [h], p_hg, [w["hg"]], [BRANCH] * 4)
        o_hg, hg_states = hgrn_fwd(hq, hk, hv, hg)
        branch_o = [o_sb, o_gd, o_hg]
        y = None
        for b in range(N_BRANCHES):
            rows = [h, branch_o[b]] + ([] if y is None else [y])
            (y,) = stage_fwd(f"merge{b}", make_f_merge(b > 0, y is not None), rows, [nw, out_norms[b]],
                             [w["z"][b], w["mix"][b], w["br"][b]], [D_MODEL])
        (h_next,) = stage_fwd("out_proj", f_out, [h, y], [], [w["out"]], [D_MODEL])
        saved.append(dict(h=h, w=w, p_sb=p_sb, p_gd=p_gd, p_hg=p_hg, out_norms=out_norms, y=y,
                          sb=(sq, sk, sv, o_sb), gd=(gq, gk, gv, bg, gd_states, o_gd),
                          hg=(hq, hk, hv, hg, hg_states, o_hg)))
        h = h_next

    sq_err, dy = loss_head(h, target)
    dh = jnp.concatenate([jnp.zeros((FRONT, D_MODEL), F32), dy], axis=0)

    g = dict(norm_w=[], w_in=[], sb_qn=[], sb_kn=[], conv_w=[], a_log=[], dt_bias=[], gd_on=[], lb=[], hg_on=[],
             w_branch=[], w_out=[])
    for l in reversed(range(depth)):
        s = saved[l]
        w, hl = s["w"], s["h"]
        nw = s["p_sb"][0]
        (dy_,), _, (d_wout,) = stage_vjp("out_proj_b", f_out, [hl, s["y"]], [], [w["out"]], [dh], [1])
        d_norm = jnp.zeros((1, D_MODEL), F32)
        d_o, d_on, d_wz, d_wmix, d_wbr = [None] * 3, [None] * 3, [None] * 3, [None] * 3, [None] * 3
        branch_o = [s["sb"][3], s["gd"][5], s["hg"][5]]
        for b in reversed(range(N_BRANCHES)):
            (dh, d_o[b]), (dn, d_on[b]), (d_wz[b], d_wmix[b], d_wbr[b]) = stage_vjp(
                f"merge{b}_b", make_f_merge(b > 0, False), [hl, branch_o[b]], [nw, s["out_norms"][b]],
                [w["z"][b], w["mix"][b], w["br"][b]], [dy_], [0, 1], acc_in=dh)
            d_norm = d_norm + dn
        hq, hk, hv, hg, hg_states, _ = s["hg"]
        d_hg = hgrn_bwd(hq, hk, hv, hg, hg_states, d_o[2])
        (dh,), (dn, d_lb), (d_whg,) = stage_vjp("hg_pre_b", f_hg_pre, [hl], s["p_hg"], [w["hg"]], list(d_hg), [0],
                                                acc_in=dh)
        d_norm = d_norm + dn
        gq, gk, gv, bg, gd_states, _ = s["gd"]
        d_gd = gdn_bwd(gq, gk, gv, bg, gd_states, d_o[1])
        (dh,), (dn, d_conv, d_alog, d_dtb), (d_wgd,) = stage_vjp(
            "gd_pre_b", f_gd_pre, [hl], s["p_gd"], [w["gd"]], list(d_gd), [0], halo=True, acc_in=dh)
        d_norm = d_norm + dn
        sq, sk, sv, o_sb = s["sb"]
        d_sb = sb_bwd(sq, sk, sv, o_sb, d_o[0])
        (dh,), (dn, d_qn, d_kn), (d_wsb,) = stage_vjp("sb_pre_b", f_sb_pre, [hl], s["p_sb"], [w["sb"]], list(d_sb),
                                                      [0], acc_in=dh)
        d_norm = d_norm + dn
        g["norm_w"].append(d_norm[0])
        g["w_in"].append(jnp.concatenate(
            [d_wsb, d_wz[0], d_wgd[:, :3 * BRANCH], d_wz[1], d_wgd[:, 3 * BRANCH:3 * BRANCH + 2 * N_HEADS],
             d_whg, d_wz[2]] + d_wmix, axis=1))
        g["sb_qn"].append(d_qn[0])
        g["sb_kn"].append(d_kn[0])
        g["conv_w"].append(d_conv)
        g["a_log"].append(d_alog[0, N_HEADS:2 * N_HEADS])
        g["dt_bias"].append(d_dtb[0, N_HEADS:2 * N_HEADS])
        g["gd_on"].append(d_on[1][0])
        g["hg_on"].append(d_on[2][0])
        g["lb"].append(d_lb[0])
        g["w_branch"].append(jnp.stack(d_wbr))
        g["w_out"].append(d_wout)
    g = {k: jnp.stack(v[::-1]) for k, v in g.items()}
    (g["lb_logits"],) = lb_vjp(g.pop("lb"))
    return sq_err, dh[FRONT:], dh[PAD_FRONT:FRONT], g


ANY = pl.BlockSpec(memory_space=pl.ANY)


def _coords():
    return lax.axis_index("x"), lax.axis_index("y"), lax.axis_index("c")


def _other_chips(x, y):
    return [(2 * px + py, (px, py)) for px, py in ((1 - x, y), (x, 1 - y), (1 - x, 1 - y))]


def gather_over_chips(shards):
    na = len(shards)

    def body(*refs):
        ins, outs = refs[:na], refs[na:2 * na]
        send_sems, recv_sems, local_sems = refs[2 * na:]
        x, y, c = _coords()
        me = 2 * x + y
        copies = []
        for a in range(na):
            own = pltpu.make_async_copy(ins[a], outs[a].at[me], local_sems.at[a])
            own.start()
            copies.append(own)
        sends = []
        for a in range(na):
            for k, (_, (px, py)) in enumerate(_other_chips(x, y)):
                cp = pltpu.make_async_remote_copy(
                    src_ref=ins[a], dst_ref=outs[a].at[me], send_sem=send_sems.at[a, k], recv_sem=recv_sems.at[a, k],
                    device_id=(px, py, c), device_id_type=MESH)
                cp.start()
                sends.append(cp)
        for a in range(na):
            for k, (p, (px, py)) in enumerate(_other_chips(x, y)):
                pltpu.make_async_remote_copy(
                    src_ref=ins[a], dst_ref=outs[a].at[p], send_sem=send_sems.at[a, k], recv_sem=recv_sems.at[a, k],
                    device_id=(px, py, c), device_id_type=MESH).wait_recv()
        for cp in sends:
            cp.wait_send()
        for cp in copies:
            cp.wait()

    return pl.pallas_call(
        body, in_specs=[ANY] * na, out_specs=[ANY] * na,
        out_shape=[jax.ShapeDtypeStruct((4,) + s.shape, s.dtype) for s in shards],
        scratch_shapes=[pltpu.SemaphoreType.DMA((na, 3)), pltpu.SemaphoreType.DMA((na, 3)),
                        pltpu.SemaphoreType.DMA((na,))],
        name="gather_over_chips",
    )(*shards)


def scatter_over_chips(parts):
    na = len(parts)

    def body(*refs):
        ins, outs = refs[:na], refs[na:2 * na]
        send_sems, recv_sems, local_sems = refs[2 * na:]
        x, y, c = _coords()
        me = 2 * x + y
        copies = []
        for a in range(na):
            own = pltpu.make_async_copy(ins[a].at[me], outs[a].at[me], local_sems.at[a])
            own.start()
            copies.append(own)
        sends = []
        for a in range(na):
            for k, (p, (px, py)) in enumerate(_other_chips(x, y)):
                cp = pltpu.make_async_remote_copy(
                    src_ref=ins[a].at[p], dst_ref=outs[a].at[me], send_sem=send_sems.at[a, k],
                    recv_sem=recv_sems.at[a, k], device_id=(px, py, c), device_id_type=MESH)
                cp.start()
                sends.append(cp)
        for a in range(na):
            for k, (p, (px, py)) in enumerate(_other_chips(x, y)):
                pltpu.make_async_remote_copy(
                    src_ref=ins[a].at[p], dst_ref=outs[a].at[p], send_sem=send_sems.at[a, k],
                    recv_sem=recv_sems.at[a, k], device_id=(px, py, c), device_id_type=MESH).wait_recv()
        for cp in sends:
            cp.wait_send()
        for cp in copies:
            cp.wait()

    return pl.pallas_call(
        body, in_specs=[ANY] * na, out_specs=[ANY] * na,
        out_shape=[jax.ShapeDtypeStruct(s.shape, s.dtype) for s in parts],
        scratch_shapes=[pltpu.SemaphoreType.DMA((na, 3)), pltpu.SemaphoreType.DMA((na, 3)),
                        pltpu.SemaphoreType.DMA((na,))],
        name="scatter_over_chips",
    )(*parts)


def swap_with_sibling(arrs):
    na = len(arrs)

    def body(*refs):
        ins, outs = refs[:na], refs[na:2 * na]
        send_sems, recv_sems = refs[2 * na:]
        x, y, c = _coords()
        cps = [pltpu.make_async_remote_copy(src_ref=ins[a], dst_ref=outs[a], send_sem=send_sems.at[a],
                                            recv_sem=recv_sems.at[a], device_id=(x, y, 1 - c), device_id_type=MESH)
               for a in range(na)]
        for cp in cps:
            cp.start()
        for cp in cps:
            cp.wait()

    return pl.pallas_call(
        body, in_specs=[ANY] * na, out_specs=[ANY] * na,
        out_shape=[jax.ShapeDtypeStruct(s.shape, s.dtype) for s in arrs],
        scratch_shapes=[pltpu.SemaphoreType.DMA((na,)), pltpu.SemaphoreType.DMA((na,))],
        name="swap_with_sibling",
    )(*arrs)


def gather_over_devices(block):
    def body(in_ref, out_ref, send_sems, recv_sems, local_sem):
        x, y, c = _coords()
        me = 4 * x + 2 * y + c
        own = pltpu.make_async_copy(in_ref, out_ref.at[me], local_sem)
        own.start()
        rel = [(dx, dy, dc) for dx in (0, 1) for dy in (0, 1) for dc in (0, 1)][1:]
        peers = [(1 - x if dx else x, 1 - y if dy else y, 1 - c if dc else c) for dx, dy, dc in rel]
        sends = []
        for k, peer in enumerate(peers):
            cp = pltpu.make_async_remote_copy(src_ref=in_ref, dst_ref=out_ref.at[me], send_sem=send_sems.at[k],
                                              recv_sem=recv_sems.at[k], device_id=peer, device_id_type=MESH)
            cp.start()
            sends.append(cp)
        for k, (px, py, pc) in enumerate(peers):
            pltpu.make_async_remote_copy(src_ref=in_ref, dst_ref=out_ref.at[4 * px + 2 * py + pc],
                                         send_sem=send_sems.at[k], recv_sem=recv_sems.at[k],
                                         device_id=(px, py, pc), device_id_type=MESH).wait_recv()
        for cp in sends:
            cp.wait_send()
        own.wait()

    return pl.pallas_call(
        body, in_specs=[ANY], out_specs=ANY, out_shape=jax.ShapeDtypeStruct((8,) + block.shape, block.dtype),
        scratch_shapes=[pltpu.SemaphoreType.DMA((7,)), pltpu.SemaphoreType.DMA((7,)), pltpu.SemaphoreType.DMA],
        name="gather_over_devices",
    )(block)


def _row_tile(rows, cols, n_streams):
    budget = 24 * 1024 * 1024 // (n_streams * 2 * 4 * max(cols, 128))
    best = None
    for t in range(16, rows + 1, 16):
        if rows % t == 0 and t <= budget:
            best = t
    return best if best is not None else rows


def sum_slabs(stack):
    k, rows, cols = stack.shape
    tr = _row_tile(rows, cols, k + 1)

    def body(in_ref, out_ref):
        acc = in_ref[0].astype(F32)
        for j in range(1, k):
            acc = acc + in_ref[j].astype(F32)
        out_ref[...] = acc

    return pl.pallas_call(
        body, grid=(rows // tr,), in_specs=[pl.BlockSpec((k, tr, cols), lambda i: (0, i, 0))],
        out_specs=pl.BlockSpec((tr, cols), lambda i: (i, 0)), out_shape=jax.ShapeDtypeStruct((rows, cols), F32),
        compiler_params=pltpu.CompilerParams(dimension_semantics=("arbitrary",), vmem_limit_bytes=VMEM_LIMIT),
        name="sum_slabs",
    )(stack)


def adamw(name, g_parts, w, m, v):
    rows, cols = w.shape
    k = len(g_parts)
    tr = _row_tile(rows, cols, k + 7)
    c1 = 1.0 - ADAM_B1 ** ADAM_STEP
    c2 = 1.0 - ADAM_B2 ** ADAM_STEP

    def body(*refs):
        g_refs, (w_ref, m_ref, v_ref, g_out, d_out, m_out, v_out) = refs[:k], refs[k:]
        g = g_refs[0][...]
        for r in g_refs[1:]:
            g = g + r[...]
        m_new = ADAM_B1 * m_ref[...] + (1.0 - ADAM_B1) * g
        v_new = ADAM_B2 * v_ref[...] + (1.0 - ADAM_B2) * (g * g)
        d_out[...] = -ADAM_LR * ((m_new / c1) / (jnp.sqrt(v_new / c2) + ADAM_EPS) + ADAM_WD * w_ref[...])
        g_out[...] = g
        m_out[...] = m_new
        v_out[...] = v_new

    spec = pl.BlockSpec((tr, cols), lambda i: (i, 0))
    shape = jax.ShapeDtypeStruct((rows, cols), F32)
    return pl.pallas_call(
        body, grid=(rows // tr,), in_specs=[spec] * (k + 3), out_specs=[spec] * 4, out_shape=[shape] * 4,
        compiler_params=pltpu.CompilerParams(dimension_semantics=("arbitrary",), vmem_limit_bytes=VMEM_LIMIT),
        name=name,
    )(*g_parts, w, m, v)


_SMALL = ("norm_w", "sb_q_norm", "sb_k_norm", "gdn_a_log", "gdn_dt_bias", "gdn_out_norm", "hgrn_lb_logits",
          "hgrn_out_norm")


def _pack(arrs, lead=()):
    flat = jnp.concatenate([jnp.reshape(a, (-1,)).astype(F32) for a in list(lead) + list(arrs)])
    n = flat.shape[0]
    rows = -(-n // (8 * 128)) * 8
    return jnp.pad(flat, (0, rows * 128 - n)).reshape(rows, 128)


def _unpack(packed, shapes, n_lead=0):
    flat = packed.reshape(-1)
    out, off = [], n_lead
    for s in shapes:
        n = math.prod(s)
        out.append(flat[off:off + n].reshape(s))
        off += n
    return out


def kernel(x, meta_tokens, norm_w, w_in, sb_q_norm, sb_k_norm, gdn_conv_w, gdn_a_log, gdn_dt_bias, gdn_out_norm, hgrn_lb_logits, hgrn_out_norm, w_branch, w_out, loss_target, m_meta_tokens, m_norm_w, m_w_in, m_sb_q_norm, m_sb_k_norm, m_gdn_conv_w, m_gdn_a_log, m_gdn_dt_bias, m_gdn_out_norm, m_hgrn_lb_logits, m_hgrn_out_norm, m_w_branch, m_w_out, v_meta_tokens, v_norm_w, v_w_in, v_sb_q_norm, v_sb_k_norm, v_gdn_conv_w, v_gdn_a_log, v_gdn_dt_bias, v_gdn_out_norm, v_hgrn_lb_logits, v_hgrn_out_norm, v_w_branch, v_w_out):
    depth = norm_w.shape[0]
    small_w = dict(norm_w=norm_w, sb_q_norm=sb_q_norm, sb_k_norm=sb_k_norm, gdn_a_log=gdn_a_log,
                   gdn_dt_bias=gdn_dt_bias, gdn_out_norm=gdn_out_norm, hgrn_lb_logits=hgrn_lb_logits,
                   hgrn_out_norm=hgrn_out_norm)
    small_m = dict(zip(_SMALL, (m_norm_w, m_sb_q_norm, m_sb_k_norm, m_gdn_a_log, m_gdn_dt_bias, m_gdn_out_norm,
                                m_hgrn_lb_logits, m_hgrn_out_norm)))
    small_v = dict(zip(_SMALL, (v_norm_w, v_sb_q_norm, v_sb_k_norm, v_gdn_a_log, v_gdn_dt_bias, v_gdn_out_norm,
                                v_hgrn_lb_logits, v_hgrn_out_norm)))

    w_in_g, w_br_g, w_out_g, conv_g, meta_g = gather_over_chips(
        [w_in.astype(BF16), w_branch.astype(BF16), w_out.astype(BF16), gdn_conv_w, meta_tokens])
    w_in_full = jnp.transpose(w_in_g, (1, 2, 0, 3)).reshape(depth, D_MODEL, N_IN)
    w_br_full = jnp.transpose(w_br_g, (1, 2, 3, 0, 4)).reshape(depth, N_BRANCHES, BRANCH, D_MODEL)
    w_out_full = jnp.transpose(w_out_g, (1, 0, 2, 3)).reshape(depth, D_MODEL, D_MODEL)
    conv_full = jnp.transpose(conv_g, (1, 2, 0, 3)).reshape(depth, 4, 3 * BRANCH)
    meta_full = jnp.transpose(meta_g, (1, 0, 2)).reshape(N_META, D_MODEL)

    sq_err, grad_x, d_meta, g = device_step(
        x[0], loss_target[0], meta_full, norm_w, w_in_full, sb_q_norm, sb_k_norm, conv_full, gdn_a_log, gdn_dt_bias,
        gdn_out_norm, hgrn_lb_logits, hgrn_out_norm, w_br_full, w_out_full)

    q4 = D_MODEL // 4
    n4 = N_IN // 4
    parts = [
        jnp.transpose(g["w_in"].reshape(depth, D_MODEL, 4, n4), (2, 0, 1, 3)).reshape(4, depth * D_MODEL, n4),
        jnp.transpose(g["w_branch"].reshape(depth, N_BRANCHES, BRANCH, 4, q4), (3, 0, 1, 2, 4)).reshape(4, -1, q4),
        jnp.transpose(g["w_out"].reshape(depth, 4, q4, D_MODEL), (1, 0, 2, 3)).reshape(4, depth * q4, D_MODEL),
        jnp.transpose(g["conv_w"].reshape(depth, 4, 4, 3 * BRANCH // 4), (2, 0, 1, 3)).reshape(4, depth * 4, -1),
        jnp.transpose(d_meta.reshape(N_META, 4, q4), (1, 0, 2)),
    ]
    received = scatter_over_chips([p.astype(BF16) for p in parts])
    core_sums = [sum_slabs(r) for r in received]
    sibling_sums = swap_with_sibling(core_sums)
    c = lax.axis_index("c")
    sharded = [(w_in, m_w_in, v_w_in), (w_branch, m_w_branch, v_w_branch), (w_out, m_w_out, v_w_out),
               (gdn_conv_w, m_gdn_conv_w, v_gdn_conv_w), (meta_tokens, m_meta_tokens, v_meta_tokens)]
    big = []
    for a, (w_, m_, v_) in enumerate(sharded):
        mine, other = core_sums[a], sibling_sums[a]
        first = jnp.where(c == 0, mine, other)
        second = jnp.where(c == 0, other, mine)
        shp = mine.shape
        outs = adamw(f"adamw_{a}", [first, second], w_.reshape(shp), m_.reshape(shp), v_.reshape(shp))
        big.append([o.reshape(w_.shape) for o in outs])

    small_shapes = [small_w[n].shape for n in _SMALL]
    gs = dict(norm_w=g["norm_w"], sb_q_norm=g["sb_qn"], sb_k_norm=g["sb_kn"], gdn_a_log=g["a_log"],
              gdn_dt_bias=g["dt_bias"], gdn_out_norm=g["gd_on"], hgrn_lb_logits=g["lb_logits"],
              hgrn_out_norm=g["hg_on"])
    n_lead = 128
    lead = [jnp.pad(sq_err.reshape(1), (0, n_lead - 1))]
    packed_g = gather_over_devices(_pack([gs[n] for n in _SMALL], lead))
    zeros_lead = [jnp.zeros((n_lead,), F32)]
    pw, pm, pv = (_pack([d[n] for n in _SMALL], zeros_lead) for d in (small_w, small_m, small_v))
    sg, sd, sm, sv_ = adamw("adamw_small", [packed_g[j] for j in range(8)], pw, pm, pv)
    loss = (0.5 / D_MODEL) * sg[0, 0]
    small = [dict(zip(_SMALL, _unpack(t, small_shapes, n_lead))) for t in (sg, sd, sm, sv_)]

    order = ("meta_tokens", "norm_w", "w_in", "sb_q_norm", "sb_k_norm", "gdn_conv_w", "gdn_a_log", "gdn_dt_bias",
             "gdn_out_norm", "hgrn_lb_logits", "hgrn_out_norm", "w_branch", "w_out")
    big_idx = dict(w_in=0, w_branch=1, w_out=2, gdn_conv_w=3, meta_tokens=4)
    result = [loss, grad_x[None]]
    for kind in range(4):
        for n in order:
            result.append(big[big_idx[n]][kind] if n in big_idx else small[kind][n])
    return tuple(result)
```

```python
import functools
import math

import jax
import jax.numpy as jnp
from jax import lax
from jax.experimental import pallas as pl
from jax.experimental.pallas import tpu as pltpu

F32, BF16 = jnp.float32, jnp.bfloat16
MESH = pl.DeviceIdType.MESH

D_MODEL = 1024
BRANCH = 512
HEAD = 128
N_HEADS = 4
N_BRANCHES = 3
CHUNK = 64
N_META = 16
FRONT = 128
PAD_FRONT = FRONT - N_META
EPS = 1e-6
SB_KEY_BLOCK = 640
SB_LANE_GROUP = 128
HALO = 8
ROW_TILE = 320
SB_Q_TILE = 320
GDN_TILE = 320
HGRN_SUB = 16
LOSS_TILE = 128
VMEM_LIMIT = 56 * 1024 * 1024

ADAM_LR, ADAM_B1, ADAM_B2, ADAM_EPS, ADAM_WD, ADAM_STEP = 0.001, 0.9, 0.999, 1e-08, 0.01, 10

_C_SB_QKV = (0, 1536)
_C_SB_Z = (1536, 2048)
_C_GD_QKV = (2048, 3584)
_C_GD_Z = (3584, 4096)
_C_GD_BA = (4096, 4104)
_C_HG_QFI = (4104, 5640)
_C_HG_Z = (5640, 6152)
_C_MIX = (6152, 9224)
N_IN = 9224


def _mxu(a):
    return a.astype(BF16)


def _split3(x):
    x1 = _mxu(x)
    r = x - x1.astype(F32)
    x2 = _mxu(r)
    return x1, x2, _mxu(r - x2.astype(F32))


def _stack6(a, b, ca, cb):
    a1, a2, a3 = _split3(a)
    b1, b2, b3 = _split3(b)
    return jnp.concatenate([a1, a1, a2, a1, a2, a3], axis=ca), jnp.concatenate([b1, b2, b1, b3, b2, b1], axis=cb)


def _dg(a, b, ca, cb, hi=False):
    dn = (((ca,), (cb,)), ((), ()))
    if hi:
        a, b = _stack6(a, b, ca, cb)
    return lax.dot_general(_mxu(a), _mxu(b), dn, preferred_element_type=F32)


def _make_mm(hi):
    @jax.custom_vjp
    def nn(a, b):
        return _dg(a, b, 1, 0, hi)

    @jax.custom_vjp
    def nt(a, b):
        return _dg(a, b, 1, 1, hi)

    @jax.custom_vjp
    def tn(a, b):
        return _dg(a, b, 0, 0, hi)

    nn.defvjp(lambda a, b: (nn(a, b), (a, b)), lambda r, g: (nt(g, r[1]), tn(r[0], g)))
    nt.defvjp(lambda a, b: (nt(a, b), (a, b)), lambda r, g: (nn(g, r[1]), tn(g, r[0])))
    tn.defvjp(lambda a, b: (tn(a, b), (a, b)), lambda r, g: (nt(r[1], g), nn(r[0], g)))
    return nn, nt, tn


mm_nn, mm_nt, mm_tn = _make_mm(False)
mh_nn, mh_nt, mh_tn = _make_mm(True)


@jax.custom_vjp
def _mm_w(a, w, wz):
    return _dg(a, w, 1, 0)


def _mm_w_fwd(a, w, wz):
    return _dg(a, w, 1, 0), (a, w)


def _mm_w_bwd(res, g):
    a, w = res
    return _dg(g, w, 1, 1), jnp.zeros_like(w), _dg(a, g, 0, 0)


_mm_w.defvjp(_mm_w_fwd, _mm_w_bwd)


def mmw(a, w, wz):
    return _dg(a, w, 1, 0) if wz is None else _mm_w(a, w, wz)


@functools.partial(jax.custom_vjp, nondiff_argnums=(1,))
def roll_rows(x, s):
    return pltpu.roll(x, s, 0)


roll_rows.defvjp(lambda x, s: (pltpu.roll(x, s, 0), None),
                 lambda s, r, g: (pltpu.roll(g, g.shape[0] - s, 0),))


def _sigmoid(x):
    return 0.5 * (jnp.tanh(0.5 * x) + 1.0)


def _silu(x):
    return x * _sigmoid(x)


def _softplus(x):
    return jnp.maximum(x, 0.0) + jnp.log(1.0 + jnp.exp(-jnp.abs(x)))


def _rms(x, w):
    return x * lax.rsqrt(jnp.mean(x * x, axis=-1, keepdims=True) + EPS) * w


def _l2(x):
    return x * lax.rsqrt(jnp.sum(x * x, axis=-1, keepdims=True) + EPS)


def _heads(x, fn):
    return jnp.concatenate([fn(x[:, h * HEAD:(h + 1) * HEAD]) for h in range(N_HEADS)], axis=-1)


def _valid_rows(row0, n):
    rows = row0 + lax.broadcasted_iota(jnp.int32, (n, 1), 0)
    return (rows >= PAD_FRONT).astype(F32)


def _wz(wzs, i):
    return None if wzs is None else wzs[i]


def f_sb_pre(vals, p, w, wzs, row0):
    (h,), (norm_w, qn, kn) = vals, p
    raw = mmw(_rms(h, norm_w), w[0], _wz(wzs, 0))
    q = _heads(raw[:, :BRANCH], lambda t: _rms(t, qn))
    k = _heads(raw[:, BRANCH:2 * BRANCH], lambda t: _rms(t, kn))
    return [q, k, raw[:, 2 * BRANCH:]]


def f_gd_pre(vals, p, w, wzs, row0):
    (h_ext,), (norm_w, conv_w, a_log, dt_bias) = vals, p
    tm = h_ext.shape[0] - HALO
    raw = mmw(_rms(h_ext, norm_w), w[0], _wz(wzs, 0))
    x = raw[:, :3 * BRANCH]
    y = conv_w[3:4] * x
    for i in range(3):
        y = y + conv_w[i:i + 1] * roll_rows(x, 3 - i)
    y = _silu(y[HALO:])
    gq = _heads(y[:, :BRANCH], _l2) * (HEAD ** -0.5)
    gk = _heads(y[:, BRANCH:2 * BRANCH], _l2)
    gv = y[:, 2 * BRANCH:]
    slab = raw[HALO:, 3 * BRANCH:]
    lane = lax.broadcasted_iota(jnp.int32, (1, HEAD), 1)
    beta = _sigmoid(slab) * _valid_rows(row0, tm)
    g = -jnp.exp(a_log) * _softplus(slab + dt_bias)
    bg = jnp.where(lane < N_HEADS, beta, jnp.where(lane < 2 * N_HEADS, g, 0.0))
    return [gq, gk, gv, bg]


def f_hg_pre(vals, p, w, wzs, row0):
    (h,), (norm_w, lb) = vals, p
    raw = mmw(_rms(h, norm_w), w[0], _wz(wzs, 0))
    hq = _silu(raw[:, :BRANCH])
    fp = raw[:, BRANCH:2 * BRANCH]
    forget = lb + (1.0 - lb) * _sigmoid(fp)
    hk = (1.0 - lb) * _sigmoid(-fp)
    hv = raw[:, 2 * BRANCH:] * _valid_rows(row0, h.shape[0])
    return [hq, hk, hv, jnp.log(forget)]


def _merge_u(h, o, norm_w, out_norm, w, wzs, normed):
    xn = _rms(h, norm_w)
    z = mmw(xn, w[0], _wz(wzs, 0))
    mix = mmw(xn, w[1], _wz(wzs, 1))
    if normed:
        o = _heads(o, lambda t: _rms(t, out_norm))
    return _sigmoid(mix) * mmw(o * _silu(z), w[2], _wz(wzs, 2))


def make_f_merge(normed, with_prev):
    def f(vals, p, w, wzs, row0):
        u = _merge_u(vals[0], vals[1], p[0], p[1], w, wzs, normed)
        return [vals[2] + u] if with_prev else [u]
    return f


def f_out(vals, p, w, wzs, row0):
    h, y = vals
    return [h + mmw(y, w[0], _wz(wzs, 0))]


def _full_spec(a):
    nd = a.ndim
    return pl.BlockSpec(a.shape, lambda i, _nd=nd: (0,) * _nd)


def stage_fwd(name, fn, rows, params, weights, out_widths, *, halo=False, tm=ROW_TILE):
    t_len = rows[0].shape[0]
    n = t_len // tm
    nr, npar, nw = len(rows), len(params), len(weights)

    def body(*refs):
        i = pl.program_id(0)
        refs = list(refs)
        prev_ref = refs.pop(0) if halo else None
        row_refs, refs = refs[:nr], refs[nr:]
        par_refs, refs = refs[:npar], refs[npar:]
        w_refs, out_refs = refs[:nw], refs[nw:]
        vals = [r[...] for r in row_refs]
        if halo:
            prev = jnp.where(i == 0, 0.0, prev_ref[...])
            vals[0] = jnp.concatenate([prev, vals[0]], axis=0)
        outs = fn(vals, [r[...] for r in par_refs], [r[...] for r in w_refs], None, i * tm)
        for o_ref, o in zip(out_refs, outs):
            o_ref[...] = o

    in_specs, args = [], []
    if halo:
        in_specs.append(pl.BlockSpec((HALO, rows[0].shape[1]),
                                     lambda i: (jnp.maximum(i * (tm // HALO) - 1, 0), 0)))
        args.append(rows[0])
    for r in rows:
        in_specs.append(pl.BlockSpec((tm, r.shape[1]), lambda i: (i, 0)))
        args.append(r)
    for a in list(params) + list(weights):
        in_specs.append(_full_spec(a))
        args.append(a)
    return pl.pallas_call(
        body, grid=(n,), in_specs=in_specs,
        out_specs=[pl.BlockSpec((tm, wd), lambda i: (i, 0)) for wd in out_widths],
        out_shape=[jax.ShapeDtypeStruct((t_len, wd), F32) for wd in out_widths],
        compiler_params=pltpu.CompilerParams(dimension_semantics=("arbitrary",), vmem_limit_bytes=VMEM_LIMIT),
        name=name,
    )(*args)


def stage_vjp(name, fn, rows, params, weights, cts, diff_rows, *, halo=False, acc_in=None, tm=ROW_TILE):
    t_len = rows[0].shape[0]
    n = t_len // tm
    nr, npar, nw, nct, nd = len(rows), len(params), len(weights), len(cts), len(diff_rows)
    has_acc = acc_in is not None

    def body(*refs):
        i = pl.program_id(0)
        tile = n - 1 - i
        refs = list(refs)
        prev_ref = refs.pop(0) if halo else None
        row_refs, refs = refs[:nr], refs[nr:]
        par_refs, refs = refs[:npar], refs[npar:]
        w_refs, refs = refs[:nw], refs[nw:]
        ct_refs, refs = refs[:nct], refs[nct:]
        acc_ref = refs.pop(0) if has_acc else None
        drow_refs, refs = refs[:nd], refs[nd:]
        dpar_refs, refs = refs[:npar], refs[npar:]
        dw_refs, refs = refs[:nw], refs[nw:]
        carry_ref = refs[0] if halo else None

        vals = [r[...] for r in row_refs]
        if halo:
            prev = jnp.where(tile == 0, 0.0, prev_ref[...])
            vals[0] = jnp.concatenate([prev, vals[0]], axis=0)
        pvals = [r[...] for r in par_refs]
        wvals = [r[...] for r in w_refs]
        wzs = [jnp.zeros(w.shape, F32) for w in wvals]

        def f(dvals, pv, wz):
            full = list(vals)
            for k, idx in enumerate(diff_rows):
                full[idx] = dvals[k]
            return fn(full, pv, wvals, wz, tile * tm)

        _, vjp = jax.vjp(f, [vals[k] for k in diff_rows], pvals, wzs)
        d_rows, d_par, d_w = vjp([r[...] for r in ct_refs])

        @pl.when(i == 0)
        def _():
            for r in list(dpar_refs) + list(dw_refs):
                r[...] = jnp.zeros(r.shape, F32)

        for r, g in zip(list(dpar_refs) + list(dw_refs), list(d_par) + list(d_w)):
            r[...] += g
        for k, (r, g) in enumerate(zip(drow_refs, d_rows)):
            if k == 0 and diff_rows[0] == 0:
                if halo:
                    g_ext = g
                    g = g_ext[HALO:]
                    tail = g[tm - HALO:] + jnp.where(i == 0, 0.0, carry_ref[...])
                    g = jnp.concatenate([g[:tm - HALO], tail], axis=0)
                    carry_ref[...] = g_ext[:HALO]
                if has_acc:
                    g = g + acc_ref[...]
            r[...] = g

    rev = lambda i: (n - 1 - i, 0)
    in_specs, args = [], []
    if halo:
        in_specs.append(pl.BlockSpec((HALO, rows[0].shape[1]),
                                     lambda i: (jnp.maximum((n - 1 - i) * (tm // HALO) - 1, 0), 0)))
        args.append(rows[0])
    for r in list(rows):
        in_specs.append(pl.BlockSpec((tm, r.shape[1]), rev))
        args.append(r)
    for a in list(params) + list(weights):
        in_specs.append(_full_spec(a))
        args.append(a)
    for c in cts:
        in_specs.append(pl.BlockSpec((tm, c.shape[1]), rev))
        args.append(c)
    if has_acc:
        in_specs.append(pl.BlockSpec((tm, acc_in.shape[1]), rev))
        args.append(acc_in)
    out_specs = [pl.BlockSpec((tm, rows[k].shape[1]), rev) for k in diff_rows]
    out_shape = [jax.ShapeDtypeStruct(rows[k].shape, F32) for k in diff_rows]
    for a in list(params) + list(weights):
        out_specs.append(_full_spec(a))
        out_shape.append(jax.ShapeDtypeStruct(a.shape, F32))
    scratch = [pltpu.VMEM((HALO, rows[0].shape[1]), F32)] if halo else []
    outs = pl.pallas_call(
        body, grid=(n,), in_specs=in_specs, out_specs=out_specs, out_shape=out_shape, scratch_shapes=scratch,
        compiler_params=pltpu.CompilerParams(dimension_semantics=("arbitrary",), vmem_limit_bytes=VMEM_LIMIT),
        name=name,
    )(*args)
    return outs[:nd], outs[nd:nd + npar], outs[nd + npar:]


def _sb_tile(qb, kj, t_idx, ks):
    z = _dg(qb, kj, 1, 1)
    sp = jnp.log(1.0 + jnp.exp(-jnp.abs(z)))
    ls = jnp.minimum(z, 0.0) - sp
    if t_idx is None:
        return None, ls, ls - z
    s_idx = ks + lax.broadcasted_iota(jnp.int32, (1, SB_KEY_BLOCK), 1)
    mask = (s_idx < t_idx) & (s_idx >= PAD_FRONT)
    lk = jnp.where(mask, ls - z, 0.0)
    return mask, ls, lk


def _masked(mask, x):
    return x if mask is None else jnp.where(mask, x, 0.0)


def _walk_key_blocks(n_blocks, step, carry):
    carry = step(0, carry, True)
    last = jnp.maximum(n_blocks - 1, 1)
    carry = lax.fori_loop(1, last, lambda jj, c: step(jj, c, False), carry)
    return lax.fori_loop(last, n_blocks, lambda jj, c: step(jj, c, True), carry)


def _tri(n, kind):
    r = lax.broadcasted_iota(jnp.int32, (n, n), 0)
    c = lax.broadcasted_iota(jnp.int32, (n, n), 1)
    return {"gt": r > c, "ge": r >= c, "eq": r == c}[kind]


def _suffix_sums(x, tri, carry, inclusive):
    groups = []
    for g in reversed(range(x.shape[1] // SB_LANE_GROUP)):
        xg = x[:, g * SB_LANE_GROUP:(g + 1) * SB_LANE_GROUP]
        hi = _mxu(xg)
        r1 = xg - hi.astype(F32)
        mid = _mxu(r1)
        terms = [hi, mid] if tri.shape[0] == 2 * SB_LANE_GROUP else [hi, mid, _mxu(r1 - mid.astype(F32))]
        sg = _dg(jnp.concatenate(terms, axis=1), tri, 1, 0)
        groups.append(sg + carry)
        carry = carry + (sg[:, 0:1] if inclusive else sg[:, 0:1] + xg[:, 0:1])
    return jnp.concatenate(groups[::-1], axis=1), carry


def sb_fwd(q, k, v, *, tq=SB_Q_TILE):
    t_len = q.shape[0]
    nq = t_len // tq
    kb = SB_KEY_BLOCK

    def body(q_ref, k_ref, v_ref, o_ref):
        i = pl.program_id(1)
        qb = _mxu(q_ref[...] * (HEAD ** -0.5))
        t_idx = i * tq + lax.broadcasted_iota(jnp.int32, (tq, 1), 0)
        after = _mxu(jnp.tile(_tri(SB_LANE_GROUP, "gt").astype(F32), (2, 1)))
        o_ref[...] = jnp.zeros(o_ref.shape, F32)
        n_blocks = ((i + 1) * tq + kb - 1) // kb

        def step(jj, run, masked):
            ks = pl.multiple_of((n_blocks - 1 - jj) * kb, kb)
            kj = _mxu(k_ref[pl.ds(ks, kb), :])
            vj = _mxu(v_ref[pl.ds(ks, kb), :])
            mask, ls, lk = _sb_tile(qb, kj, t_idx if masked else None, ks)
            passed, run = _suffix_sums(lk, after, run, False)
            a = _masked(mask, jnp.exp(ls + passed))
            o_ref[...] += _dg(a, vj, 1, 0)
            return run

        _walk_key_blocks(n_blocks, step, jnp.zeros((tq, 1), F32))

    col = pl.BlockSpec((t_len, HEAD), lambda h, i: (0, h))
    tile = pl.BlockSpec((tq, HEAD), lambda h, i: (i, h))
    return pl.pallas_call(
        body, grid=(N_HEADS, nq), in_specs=[tile, col, col], out_specs=tile,
        out_shape=jax.ShapeDtypeStruct((t_len, BRANCH), F32),
        compiler_params=pltpu.CompilerParams(dimension_semantics=("arbitrary", "arbitrary"),
                                             vmem_limit_bytes=VMEM_LIMIT),
        name="sb_fwd",
    )(q, k, v)


def sb_bwd(q, k, v, o, do, *, tq=SB_Q_TILE):
    t_len = q.shape[0]
    nq = t_len // tq
    kb = SB_KEY_BLOCK
    scale = HEAD ** -0.5

    def body(q_ref, k_ref, v_ref, o_ref, do_ref, dq_ref, dk_ref, dv_ref):
        i = pl.program_id(1)

        @pl.when(i == 0)
        def _():
            dk_ref[...] = jnp.zeros(dk_ref.shape, F32)
            dv_ref[...] = jnp.zeros(dv_ref.shape, F32)

        qb = _mxu(q_ref[...] * scale)
        do_f = do_ref[...]
        dob = _mxu(do_f)
        total = jnp.sum(dob.astype(F32) * o_ref[...], axis=-1, keepdims=True)
        t_idx = i * tq + lax.broadcasted_iota(jnp.int32, (tq, 1), 0)
        after = _mxu(jnp.tile(_tri(SB_LANE_GROUP, "gt").astype(F32), (2, 1)))
        from_here = _mxu(jnp.tile(_tri(SB_LANE_GROUP, "ge").astype(F32), (3, 1)))
        dq_ref[...] = jnp.zeros(dq_ref.shape, F32)
        n_blocks = ((i + 1) * tq + kb - 1) // kb

        def step(jj, carry, masked):
            run, run_dl = carry
            ks = pl.multiple_of((n_blocks - 1 - jj) * kb, kb)
            kj = _mxu(k_ref[pl.ds(ks, kb), :])
            vj = _mxu(v_ref[pl.ds(ks, kb), :])
            mask, ls, lk = _sb_tile(qb, kj, t_idx if masked else None, ks)
            passed, run = _suffix_sums(lk, after, run, False)
            a = _masked(mask, jnp.exp(ls + passed))
            ab = _mxu(a)
            dl = ab.astype(F32) * _dg(dob, vj, 1, 1)
            from_s, run_dl = _suffix_sums(dl, from_here, run_dl, True)
            before = total - from_s
            sig = jnp.exp(ls)
            dzb = _mxu(dl * (1.0 - sig) - _masked(mask, sig * before))
            dq_ref[...] += _dg(dzb, kj, 1, 0)
            dk_ref[pl.ds(ks, kb), :] += _dg(dzb, qb, 0, 0)
            dv_ref[pl.ds(ks, kb), :] += _dg(ab, dob, 0, 0)
            return run, run_dl

        zero = jnp.zeros((tq, 1), F32)
        _walk_key_blocks(n_blocks, step, (zero, zero))
        dq_ref[...] *= scale

    col = pl.BlockSpec((t_len, HEAD), lambda h, i: (0, h))
    tile = pl.BlockSpec((tq, HEAD), lambda h, i: (i, h))
    full = jax.ShapeDtypeStruct((t_len, BRANCH), F32)
    return pl.pallas_call(
        body, grid=(N_HEADS, nq), in_specs=[tile, col, col, tile, tile], out_specs=[tile, col, col],
        out_shape=[full, full, full],
        compiler_params=pltpu.CompilerParams(dimension_semantics=("arbitrary", "arbitrary"),
                                             vmem_limit_bytes=VMEM_LIMIT),
        name="sb_bwd",
    )(q, k, v, o, do)


def _bdg(a, b, ca, cb, hi=False):
    dn = (((ca,), (cb,)), ((0,), (0,)))
    if hi:
        a, b = _stack6(a, b, ca, cb)
    return lax.dot_general(_mxu(a), _mxu(b), dn, preferred_element_type=F32)


def _make_bmm(hi):
    @jax.custom_vjp
    def nn(a, b):
        return _bdg(a, b, 2, 1, hi)

    @jax.custom_vjp
    def nt(a, b):
        return _bdg(a, b, 2, 2, hi)

    @jax.custom_vjp
    def tn(a, b):
        return _bdg(a, b, 1, 1, hi)

    nn.defvjp(lambda a, b: (nn(a, b), (a, b)), lambda r, g: (nt(g, r[1]), tn(r[0], g)))
    nt.defvjp(lambda a, b: (nt(a, b), (a, b)), lambda r, g: (nn(g, r[1]), tn(g, r[0])))
    tn.defvjp(lambda a, b: (tn(a, b), (a, b)), lambda r, g: (nt(r[1], g), nn(r[0], g)))
    return nn, nt, tn


bm_nn, bm_nt, bm_tn = _make_bmm(False)
bh_nn, bh_nt, bh_tn = _make_bmm(True)


def gdn_tile(state, q, k, v, bg):
    c = CHUNK
    nc = q.shape[0] // c
    pairs = [(ci, h) for ci in range(nc) for h in range(N_HEADS)]
    nb = len(pairs)
    rows = lambda ci: slice(ci * c, (ci + 1) * c)
    split = lambda x: jnp.stack([x[rows(ci), h * HEAD:(h + 1) * HEAD] for ci, h in pairs])
    qs, ks, vs = split(q), split(k), split(v)
    beta = jnp.stack([bg[rows(ci), h:h + 1] for ci, h in pairs])
    g = jnp.stack([bg[rows(ci), N_HEADS + h:N_HEADS + h + 1] for ci, h in pairs])
    causal = jnp.broadcast_to(_tri(c, "ge")[None], (nb, c, c))
    strict = jnp.broadcast_to(_tri(c, "gt")[None], (nb, c, c))
    eye = jnp.broadcast_to(_tri(c, "eq")[None], (nb, c, c)).astype(F32)
    g_lanes = bh_nn(causal.astype(F32), g * jnp.ones((1, 1, HEAD), F32))
    g_col = g_lanes[:, :, :c]
    g_row = bh_nn(jnp.ones((nb, c, c), F32), eye * g_col)
    decay = jnp.where(causal, jnp.exp(jnp.where(causal, g_col - g_row, 0.0)), 0.0)
    kb = ks * beta
    m = jnp.where(strict, bm_nt(kb, ks) * decay, 0.0)
    inv = eye - m
    p = bh_nn(m, m)
    for level in range(5):
        inv = inv + bh_nn(inv, p)
        if level < 4:
            p = bh_nn(p, p)
    g1 = g_lanes[:, :, 0:1]
    g_last = g1[:, c - 1:c, :]
    u = bh_nn(inv, vs * beta)
    w = bh_nn(inv, kb * jnp.exp(g1))
    a_qk = jnp.where(causal, bm_nt(qs, ks) * decay, 0.0)
    qd = qs * jnp.exp(g1)
    kd = ks * jnp.exp(g_last - g1)
    gl = jnp.exp(g_last)
    outs = []
    for ci in range(nc):
        sl = slice(ci * N_HEADS, (ci + 1) * N_HEADS)
        v_new = u[sl] - bm_nn(w[sl], state)
        o = bm_nn(qd[sl], state) + bm_nn(a_qk[sl], v_new)
        state = state * gl[sl] + bm_tn(kd[sl], v_new)
        outs.append(jnp.concatenate([o[h] for h in range(N_HEADS)], axis=1))
    return state, jnp.concatenate(outs, axis=0)


def gdn_fwd(q, k, v, bg, *, tg=GDN_TILE):
    t_len = q.shape[0]
    n = t_len // tg

    def body(q_ref, k_ref, v_ref, bg_ref, o_ref, s_all_ref, s_ref):
        @pl.when(pl.program_id(0) == 0)
        def _():
            s_ref[...] = jnp.zeros(s_ref.shape, F32)

        s_in = s_ref[...]
        s_all_ref[0] = s_in
        s_out, o = gdn_tile(s_in, q_ref[...], k_ref[...], v_ref[...], bg_ref[...])
        o_ref[...] = o
        s_ref[...] = s_out

    row = lambda wd: pl.BlockSpec((tg, wd), lambda i: (i, 0))
    return pl.pallas_call(
        body, grid=(n,), in_specs=[row(BRANCH), row(BRANCH), row(BRANCH), row(HEAD)],
        out_specs=[row(BRANCH), pl.BlockSpec((1, N_HEADS, HEAD, HEAD), lambda i: (i, 0, 0, 0))],
        out_shape=[jax.ShapeDtypeStruct((t_len, BRANCH), F32),
                   jax.ShapeDtypeStruct((n, N_HEADS, HEAD, HEAD), F32)],
        scratch_shapes=[pltpu.VMEM((N_HEADS, HEAD, HEAD), F32)],
        compiler_params=pltpu.CompilerParams(dimension_semantics=("arbitrary",), vmem_limit_bytes=VMEM_LIMIT),
        name="gdn_fwd",
    )(q, k, v, bg)


def gdn_bwd(q, k, v, bg, s_all, do, *, tg=GDN_TILE):
    t_len = q.shape[0]
    n = t_len // tg

    def body(q_ref, k_ref, v_ref, bg_ref, s_all_ref, do_ref, dq_ref, dk_ref, dv_ref, dbg_ref, ds_ref):
        @pl.when(pl.program_id(0) == 0)
        def _():
            ds_ref[...] = jnp.zeros(ds_ref.shape, F32)

        _, vjp = jax.vjp(gdn_tile, s_all_ref[0], q_ref[...], k_ref[...], v_ref[...], bg_ref[...])
        ds, dq, dk, dv, dbg = vjp((ds_ref[...], do_ref[...]))
        ds_ref[...] = ds
        dq_ref[...] = dq
        dk_ref[...] = dk
        dv_ref[...] = dv
        dbg_ref[...] = dbg

    row = lambda wd: pl.BlockSpec((tg, wd), lambda i: (n - 1 - i, 0))
    wide, slab = jax.ShapeDtypeStruct((t_len, BRANCH), F32), jax.ShapeDtypeStruct((t_len, HEAD), F32)
    return pl.pallas_call(
        body, grid=(n,),
        in_specs=[row(BRANCH), row(BRANCH), row(BRANCH), row(HEAD),
                  pl.BlockSpec((1, N_HEADS, HEAD, HEAD), lambda i: (n - 1 - i, 0, 0, 0)), row(BRANCH)],
        out_specs=[row(BRANCH), row(BRANCH), row(BRANCH), row(HEAD)],
        out_shape=[wide, wide, wide, slab],
        scratch_shapes=[pltpu.VMEM((N_HEADS, HEAD, HEAD), F32)],
        compiler_params=pltpu.CompilerParams(dimension_semantics=("arbitrary",), vmem_limit_bytes=VMEM_LIMIT),
        name="gdn_bwd",
    )(q, k, v, bg, s_all, do)


def hgrn_chunk(state_t, q, k, v, g):
    c, sub = CHUNK, HGRN_SUB
    nsub = c // sub
    gc = mh_nn(_tri(c, "ge").astype(F32), g)
    o = mm_nt(q * jnp.exp(gc), state_t)
    blk = lambda x, b: x[b * sub:(b + 1) * sub]
    anchors = [gc[b * sub:b * sub + 1] for b in range(nsub)]
    q_anch = jnp.stack([blk(q, b) * jnp.exp(blk(gc, b) - anchors[b]) for b in range(nsub)])
    k_anch = jnp.stack([k * jnp.exp(jnp.minimum(anchors[b] - gc, 0.0)) for b in range(nsub)])
    a_far = bm_nt(q_anch, k_anch)
    a_far = jnp.concatenate([a_far[b] for b in range(nsub)], axis=0)
    r_i = lax.broadcasted_iota(jnp.int32, (c, c), 0)
    c_i = lax.broadcasted_iota(jnp.int32, (c, c), 1)
    o = o + mm_nn(jnp.where(c_i < jnp.bitwise_and(r_i, -sub), a_far, 0.0), v)
    row = lax.broadcasted_iota(jnp.int32, (c, 1), 0)
    for d in range(sub):
        kr, gr, vr = (k, gc, v) if d == 0 else (roll_rows(k, d), roll_rows(gc, d), roll_rows(v, d))
        ok = jnp.bitwise_and(row, sub - 1) >= d
        e = jnp.exp(jnp.where(ok, gc - gr, 0.0))
        a = jnp.where(ok, jnp.sum(q * kr * e, axis=-1, keepdims=True), 0.0)
        o = o + a * vr
    g_end = gc[c - 1:c, :]
    state_t = state_t * jnp.exp(g_end) + mm_tn(v, k * jnp.exp(g_end - gc))
    return state_t, o


def hgrn_fwd(q, k, v, g, *, tg=ROW_TILE):
    t_len = q.shape[0]
    n, nc = t_len // tg, tg // CHUNK

    def body(q_ref, k_ref, v_ref, g_ref, o_ref, s_all_ref, s_ref):
        @pl.when(pl.program_id(1) == 0)
        def _():
            s_ref[...] = jnp.zeros(s_ref.shape, F32)

        def step(c, _):
            r = pl.ds(pl.multiple_of(c * CHUNK, CHUNK), CHUNK)
            s_in = s_ref[...]
            s_all_ref[c, 0] = s_in
            s_out, o = hgrn_chunk(s_in, q_ref[r, :], k_ref[r, :], v_ref[r, :], g_ref[r, :])
            o_ref[r, :] = o
            s_ref[...] = s_out
            return 0

        lax.fori_loop(0, nc, step, 0)

    row = pl.BlockSpec((tg, HEAD), lambda h, i: (i, h))
    return pl.pallas_call(
        body, grid=(N_HEADS, n), in_specs=[row, row, row, row],
        out_specs=[row, pl.BlockSpec((nc, 1, HEAD, HEAD), lambda h, i: (i, h, 0, 0))],
        out_shape=[jax.ShapeDtypeStruct((t_len, BRANCH), F32),
                   jax.ShapeDtypeStruct((t_len // CHUNK, N_HEADS, HEAD, HEAD), F32)],
        scratch_shapes=[pltpu.VMEM((HEAD, HEAD), F32)],
        compiler_params=pltpu.CompilerParams(dimension_semantics=("arbitrary", "arbitrary"),
                                             vmem_limit_bytes=VMEM_LIMIT),
        name="hgrn_fwd",
    )(q, k, v, g)


def hgrn_bwd(q, k, v, g, s_all, do, *, tg=ROW_TILE):
    t_len = q.shape[0]
    n, nc = t_len // tg, tg // CHUNK

    def body(q_ref, k_ref, v_ref, g_ref, s_all_ref, do_ref, dq_ref, dk_ref, dv_ref, dg_ref, ds_ref):
        @pl.when(pl.program_id(1) == 0)
        def _():
            ds_ref[...] = jnp.zeros(ds_ref.shape, F32)

        def step(cc, _):
            c = nc - 1 - cc
            r = pl.ds(pl.multiple_of(c * CHUNK, CHUNK), CHUNK)
            _, vjp = jax.vjp(hgrn_chunk, s_all_ref[c, 0], q_ref[r, :], k_ref[r, :], v_ref[r, :], g_ref[r, :])
            ds, dq, dk, dv, dg = vjp((ds_ref[...], do_ref[r, :]))
            ds_ref[...] = ds
            dq_ref[r, :] = dq
            dk_ref[r, :] = dk
            dv_ref[r, :] = dv
            dg_ref[r, :] = dg
            return 0

        lax.fori_loop(0, nc, step, 0)

    row = pl.BlockSpec((tg, HEAD), lambda h, i: (n - 1 - i, h))
    wide = jax.ShapeDtypeStruct((t_len, BRANCH), F32)
    return pl.pallas_call(
        body, grid=(N_HEADS, n),
        in_specs=[row, row, row, row, pl.BlockSpec((nc, 1, HEAD, HEAD), lambda h, i: (n - 1 - i, h, 0, 0)), row],
        out_specs=[row, row, row, row], out_shape=[wide, wide, wide, wide],
        scratch_shapes=[pltpu.VMEM((HEAD, HEAD), F32)],
        compiler_params=pltpu.CompilerParams(dimension_semantics=("arbitrary", "arbitrary"),
                                             vmem_limit_bytes=VMEM_LIMIT),
        name="hgrn_bwd",
    )(q, k, v, g, s_all, do)


def loss_head(h, target, *, tl=LOSS_TILE):
    seq = target.shape[0]
    n = seq // tl
    off = FRONT // tl

    def body(h_ref, t_ref, sq_ref, dy_ref):
        @pl.when(pl.program_id(0) == 0)
        def _():
            sq_ref[...] = jnp.zeros(sq_ref.shape, F32)

        err = h_ref[...] - t_ref[...]
        sq_ref[...] += jnp.sum(err * err, keepdims=True)
        dy_ref[...] = err * (1.0 / D_MODEL)

    return pl.pallas_call(
        body, grid=(n,),
        in_specs=[pl.BlockSpec((tl, D_MODEL), lambda i: (i + off, 0)), pl.BlockSpec((tl, D_MODEL), lambda i: (i, 0))],
        out_specs=[pl.BlockSpec((1, 1), lambda i: (0, 0)), pl.BlockSpec((tl, D_MODEL), lambda i: (i, 0))],
        out_shape=[jax.ShapeDtypeStruct((1, 1), F32), jax.ShapeDtypeStruct((seq, D_MODEL), F32)],
        compiler_params=pltpu.CompilerParams(dimension_semantics=("arbitrary",)),
        name="loss_head",
    )(h, target)


def _pad_lanes(a, lo, n=HEAD):
    return jnp.pad(a.astype(F32), (lo, n - lo - a.shape[0])).reshape(1, n)


def _lower_bounds(logits):
    p = jax.nn.softmax(logits.astype(F32), axis=0)
    return jnp.cumsum(p, axis=0) - p[0:1]


def _layer_weights(w_in_l, w_branch_l, w_out_l):
    c = lambda r: w_in_l[:, r[0]:r[1]]
    ba = jnp.pad(c(_C_GD_BA), ((0, 0), (0, HEAD - 2 * N_HEADS)))
    mix = c(_C_MIX)
    return dict(
        sb=c(_C_SB_QKV), gd=jnp.concatenate([c(_C_GD_QKV), ba], axis=1), hg=c(_C_HG_QFI),
        z=[c(_C_SB_Z), c(_C_GD_Z), c(_C_HG_Z)],
        mix=[mix[:, b * D_MODEL:(b + 1) * D_MODEL] for b in range(N_BRANCHES)],
        br=[w_branch_l[b] for b in range(N_BRANCHES)], out=w_out_l)


def device_step(x, target, meta, norm_w, w_in, sb_qn, sb_kn, conv_w, a_log, dt_bias, gd_on, lb_logits, hg_on,
                w_branch, w_out):
    depth = norm_w.shape[0]
    lbs, lb_vjp = jax.vjp(_lower_bounds, lb_logits)
    h = jnp.concatenate([jnp.zeros((PAD_FRONT, D_MODEL), F32), meta, x], axis=0)
    row = lambda a: a.reshape(1, -1)
    saved = []
    for l in range(depth):
        w = _layer_weights(w_in[l], w_branch[l], w_out[l])
        nw = row(norm_w[l])
        p_sb = [nw, row(sb_qn[l]), row(sb_kn[l])]
        p_gd = [nw, conv_w[l], _pad_lanes(a_log[l], N_HEADS), _pad_lanes(dt_bias[l], N_HEADS)]
        p_hg = [nw, row(lbs[l])]
        out_norms = [row(sb_qn[l]), row(gd_on[l]), row(hg_on[l])]
        sq, sk, sv = stage_fwd("sb_pre", f_sb_pre, [h], p_sb, [w["sb"]], [BRANCH] * 3)
        o_sb = sb_fwd(sq, sk, sv)
        gq, gk, gv, bg = stage_fwd("gd_pre", f_gd_pre, [h], p_gd, [w["gd"]], [BRANCH] * 3 + [HEAD], halo=True)
        o_gd, gd_states = gdn_fwd(gq, gk, gv, bg)
        hq, hk, hv, hg = stage_fwd("hg_pre", f_hg_pre, [h], p_hg, [w["hg"]], [BRANCH] * 4)
        o_hg, hg_states = hgrn_fwd(hq, hk, hv, hg)
        branch_o = [o_sb, o_gd, o_hg]
        y = None
        for b in range(N_BRANCHES):
            rows = [h, branch_o[b]] + ([] if y is None else [y])
            (y,) = stage_fwd(f"merge{b}", make_f_merge(b > 0, y is not None), rows, [nw, out_norms[b]],
                             [w["z"][b], w["mix"][b], w["br"][b]], [D_MODEL])
        (h_next,) = stage_fwd("out_proj", f_out, [h, y], [], [w["out"]], [D_MODEL])
        saved.append(dict(h=h, w=w, p_sb=p_sb, p_gd=p_gd, p_hg=p_hg, out_norms=out_norms, y=y,
                          sb=(sq, sk, sv, o_sb), gd=(gq, gk, gv, bg, gd_states, o_gd),
                          hg=(hq, hk, hv, hg, hg_states, o_hg)))
        h = h_next

    sq_err, dy = loss_head(h, target)
    dh = jnp.concatenate([jnp.zeros((FRONT, D_MODEL), F32), dy], axis=0)

    g = dict(norm_w=[], w_in=[], sb_qn=[], sb_kn=[], conv_w=[], a_log=[], dt_bias=[], gd_on=[], lb=[], hg_on=[],
             w_branch=[], w_out=[])
    for l in reversed(range(depth)):
        s = saved[l]
        w, hl = s["w"], s["h"]
        nw = s["p_sb"][0]
        (dy_,), _, (d_wout,) = stage_vjp("out_proj_b", f_out, [hl, s["y"]], [], [w["out"]], [dh], [1])
        d_norm = jnp.zeros((1, D_MODEL), F32)
        d_o, d_on, d_wz, d_wmix, d_wbr = [None] * 3, [None] * 3, [None] * 3, [None] * 3, [None] * 3
        branch_o = [s["sb"][3], s["gd"][5], s["hg"][5]]
        for b in reversed(range(N_BRANCHES)):
            (dh, d_o[b]), (dn, d_on[b]), (d_wz[b], d_wmix[b], d_wbr[b]) = stage_vjp(
                f"merge{b}_b", make_f_merge(b > 0, False), [hl, branch_o[b]], [nw, s["out_norms"][b]],
                [w["z"][b], w["mix"][b], w["br"][b]], [dy_], [0, 1], acc_in=dh)
            d_norm = d_norm + dn
        hq, hk, hv, hg, hg_states, _ = s["hg"]
        d_hg = hgrn_bwd(hq, hk, hv, hg, hg_states, d_o[2])
        (dh,), (dn, d_lb), (d_whg,) = stage_vjp("hg_pre_b", f_hg_pre, [hl], s["p_hg"], [w["hg"]], list(d_hg), [0],
                                                acc_in=dh)
        d_norm = d_norm + dn
        gq, gk, gv, bg, gd_states, _ = s["gd"]
        d_gd = gdn_bwd(gq, gk, gv, bg, gd_states, d_o[1])
        (dh,), (dn, d_conv, d_alog, d_dtb), (d_wgd,) = stage_vjp(
            "gd_pre_b", f_gd_pre, [hl], s["p_gd"], [w["gd"]], list(d_gd), [0], halo=True, acc_in=dh)
        d_norm = d_norm + dn
        sq, sk, sv, o_sb = s["sb"]
        d_sb = sb_bwd(sq, sk, sv, o_sb, d_o[0])
        (dh,), (dn, d_qn, d_kn), (d_wsb,) = stage_vjp("sb_pre_b", f_sb_pre, [hl], s["p_sb"], [w["sb"]], list(d_sb),
                                                      [0], acc_in=dh)
        d_norm = d_norm + dn
        g["norm_w"].append(d_norm[0])
        g["w_in"].append(jnp.concatenate(
            [d_wsb, d_wz[0], d_wgd[:, :3 * BRANCH], d_wz[1], d_wgd[:, 3 * BRANCH:3 * BRANCH + 2 * N_HEADS],
             d_whg, d_wz[2]] + d_wmix, axis=1))
        g["sb_qn"].append(d_qn[0])
        g["sb_kn"].append(d_kn[0])
        g["conv_w"].append(d_conv)
        g["a_log"].append(d_alog[0, N_HEADS:2 * N_HEADS])
        g["dt_bias"].append(d_dtb[0, N_HEADS:2 * N_HEADS])
        g["gd_on"].append(d_on[1][0])
        g["hg_on"].append(d_on[2][0])
        g["lb"].append(d_lb[0])
        g["w_branch"].append(jnp.stack(d_wbr))
        g["w_out"].append(d_wout)
    g = {k: jnp.stack(v[::-1]) for k, v in g.items()}
    (g["lb_logits"],) = lb_vjp(g.pop("lb"))
    return sq_err, dh[FRONT:], dh[PAD_FRONT:FRONT], g


ANY = pl.BlockSpec(memory_space=pl.ANY)


def _coords():
    return lax.axis_index("x"), lax.axis_index("y"), lax.axis_index("c")


def _other_chips(x, y):
    return [(2 * px + py, (px, py)) for px, py in ((1 - x, y), (x, 1 - y), (1 - x, 1 - y))]


def gather_over_chips(shards):
    na = len(shards)

    def body(*refs):
        ins, outs = refs[:na], refs[na:2 * na]
        send_sems, recv_sems, local_sems = refs[2 * na:]
        x, y, c = _coords()
        me = 2 * x + y
        copies = []
        for a in range(na):
            own = pltpu.make_async_copy(ins[a], outs[a].at[me], local_sems.at[a])
            own.start()
            copies.append(own)
        sends = []
        for a in range(na):
            for k, (_, (px, py)) in enumerate(_other_chips(x, y)):
                cp = pltpu.make_async_remote_copy(
                    src_ref=ins[a], dst_ref=outs[a].at[me], send_sem=send_sems.at[a, k], recv_sem=recv_sems.at[a, k],
                    device_id=(px, py, c), device_id_type=MESH)
                cp.start()
                sends.append(cp)
        for a in range(na):
            for k, (p, (px, py)) in enumerate(_other_chips(x, y)):
                pltpu.make_async_remote_copy(
                    src_ref=ins[a], dst_ref=outs[a].at[p], send_sem=send_sems.at[a, k], recv_sem=recv_sems.at[a, k],
                    device_id=(px, py, c), device_id_type=MESH).wait_recv()
        for cp in sends:
            cp.wait_send()
        for cp in copies:
            cp.wait()

    return pl.pallas_call(
        body, in_specs=[ANY] * na, out_specs=[ANY] * na,
        out_shape=[jax.ShapeDtypeStruct((4,) + s.shape, s.dtype) for s in shards],
        scratch_shapes=[pltpu.SemaphoreType.DMA((na, 3)), pltpu.SemaphoreType.DMA((na, 3)),
                        pltpu.SemaphoreType.DMA((na,))],
        name="gather_over_chips",
    )(*shards)


def scatter_over_chips(parts):
    na = len(parts)

    def body(*refs):
        ins, outs = refs[:na], refs[na:2 * na]
        send_sems, recv_sems, local_sems = refs[2 * na:]
        x, y, c = _coords()
        me = 2 * x + y
        copies = []
        for a in range(na):
            own = pltpu.make_async_copy(ins[a].at[me], outs[a].at[me], local_sems.at[a])
            own.start()
            copies.append(own)
        sends = []
        for a in range(na):
            for k, (p, (px, py)) in enumerate(_other_chips(x, y)):
                cp = pltpu.make_async_remote_copy(
                    src_ref=ins[a].at[p], dst_ref=outs[a].at[me], send_sem=send_sems.at[a, k],
                    recv_sem=recv_sems.at[a, k], device_id=(px, py, c), device_id_type=MESH)
                cp.start()
                sends.append(cp)
        for a in range(na):
            for k, (p, (px, py)) in enumerate(_other_chips(x, y)):
                pltpu.make_async_remote_copy(
                    src_ref=ins[a].at[p], dst_ref=outs[a].at[p], send_sem=send_sems.at[a, k],
                    recv_sem=recv_sems.at[a, k], device_id=(px, py, c), device_id_type=MESH).wait_recv()
        for cp in sends:
            cp.wait_send()
        for cp in copies:
            cp.wait()

    return pl.pallas_call(
        body, in_specs=[ANY] * na, out_specs=[ANY] * na,
        out_shape=[jax.ShapeDtypeStruct(s.shape, s.dtype) for s in parts],
        scratch_shapes=[pltpu.SemaphoreType.DMA((na, 3)), pltpu.SemaphoreType.DMA((na, 3)),
                        pltpu.SemaphoreType.DMA((na,))],
        name="scatter_over_chips",
    )(*parts)


def swap_with_sibling(arrs):
    na = len(arrs)

    def body(*refs):
        ins, outs = refs[:na], refs[na:2 * na]
        send_sems, recv_sems = refs[2 * na:]
        x, y, c = _coords()
        cps = [pltpu.make_async_remote_copy(src_ref=ins[a], dst_ref=outs[a], send_sem=send_sems.at[a],
                                            recv_sem=recv_sems.at[a], device_id=(x, y, 1 - c), device_id_type=MESH)
               for a in range(na)]
        for cp in cps:
            cp.start()
        for cp in cps:
            cp.wait()

    return pl.pallas_call(
        body, in_specs=[ANY] * na, out_specs=[ANY] * na,
        out_shape=[jax.ShapeDtypeStruct(s.shape, s.dtype) for s in arrs],
        scratch_shapes=[pltpu.SemaphoreType.DMA((na,)), pltpu.SemaphoreType.DMA((na,))],
        name="swap_with_sibling",
    )(*arrs)


def gather_over_devices(block):
    def body(in_ref, out_ref, send_sems, recv_sems, local_sem):
        x, y, c = _coords()
        me = 4 * x + 2 * y + c
        own = pltpu.make_async_copy(in_ref, out_ref.at[me], local_sem)
        own.start()
        rel = [(dx, dy, dc) for dx in (0, 1) for dy in (0, 1) for dc in (0, 1)][1:]
        peers = [(1 - x if dx else x, 1 - y if dy else y, 1 - c if dc else c) for dx, dy, dc in rel]
        sends = []
        for k, peer in enumerate(peers):
            cp = pltpu.make_async_remote_copy(src_ref=in_ref, dst_ref=out_ref.at[me], send_sem=send_sems.at[k],
                                              recv_sem=recv_sems.at[k], device_id=peer, device_id_type=MESH)
            cp.start()
            sends.append(cp)
        for k, (px, py, pc) in enumerate(peers):
            pltpu.make_async_remote_copy(src_ref=in_ref, dst_ref=out_ref.at[4 * px + 2 * py + pc],
                                         send_sem=send_sems.at[k], recv_sem=recv_sems.at[k],
                                         device_id=(px, py, pc), device_id_type=MESH).wait_recv()
        for cp in sends:
            cp.wait_send()
        own.wait()

    return pl.pallas_call(
        body, in_specs=[ANY], out_specs=ANY, out_shape=jax.ShapeDtypeStruct((8,) + block.shape, block.dtype),
        scratch_shapes=[pltpu.SemaphoreType.DMA((7,)), pltpu.SemaphoreType.DMA((7,)), pltpu.SemaphoreType.DMA],
        name="gather_over_devices",
    )(block)


def _row_tile(rows, cols, n_streams):
    budget = 24 * 1024 * 1024 // (n_streams * 2 * 4 * max(cols, 128))
    best = None
    for t in range(16, rows + 1, 16):
        if rows % t == 0 and t <= budget:
            best = t
    return best if best is not None else rows


def sum_slabs(stack):
    k, rows, cols = stack.shape
    tr = _row_tile(rows, cols, k + 1)

    def body(in_ref, out_ref):
        acc = in_ref[0].astype(F32)
        for j in range(1, k):
            acc = acc + in_ref[j].astype(F32)
        out_ref[...] = acc

    return pl.pallas_call(
        body, grid=(rows // tr,), in_specs=[pl.BlockSpec((k, tr, cols), lambda i: (0, i, 0))],
        out_specs=pl.BlockSpec((tr, cols), lambda i: (i, 0)), out_shape=jax.ShapeDtypeStruct((rows, cols), F32),
        compiler_params=pltpu.CompilerParams(dimension_semantics=("arbitrary",), vmem_limit_bytes=VMEM_LIMIT),
        name="sum_slabs",
    )(stack)


def adamw(name, g_parts, w, m, v):
    rows, cols = w.shape
    k = len(g_parts)
    tr = _row_tile(rows, cols, k + 7)
    c1 = 1.0 - ADAM_B1 ** ADAM_STEP
    c2 = 1.0 - ADAM_B2 ** ADAM_STEP

    def body(*refs):
        g_refs, (w_ref, m_ref, v_ref, g_out, d_out, m_out, v_out) = refs[:k], refs[k:]
        g = g_refs[0][...]
        for r in g_refs[1:]:
            g = g + r[...]
        m_new = ADAM_B1 * m_ref[...] + (1.0 - ADAM_B1) * g
        v_new = ADAM_B2 * v_ref[...] + (1.0 - ADAM_B2) * (g * g)
        d_out[...] = -ADAM_LR * ((m_new / c1) / (jnp.sqrt(v_new / c2) + ADAM_EPS) + ADAM_WD * w_ref[...])
        g_out[...] = g
        m_out[...] = m_new
        v_out[...] = v_new

    spec = pl.BlockSpec((tr, cols), lambda i: (i, 0))
    shape = jax.ShapeDtypeStruct((rows, cols), F32)
    return pl.pallas_call(
        body, grid=(rows // tr,), in_specs=[spec] * (k + 3), out_specs=[spec] * 4, out_shape=[shape] * 4,
        compiler_params=pltpu.CompilerParams(dimension_semantics=("arbitrary",), vmem_limit_bytes=VMEM_LIMIT),
        name=name,
    )(*g_parts, w, m, v)


_SMALL = ("norm_w", "sb_q_norm", "sb_k_norm", "gdn_a_log", "gdn_dt_bias", "gdn_out_norm", "hgrn_lb_logits",
          "hgrn_out_norm")


def _pack(arrs, lead=()):
    flat = jnp.concatenate([jnp.reshape(a, (-1,)).astype(F32) for a in list(lead) + list(arrs)])
    n = flat.shape[0]
    rows = -(-n // (8 * 128)) * 8
    return jnp.pad(flat, (0, rows * 128 - n)).reshape(rows, 128)


def _unpack(packed, shapes, n_lead=0):
    flat = packed.reshape(-1)
    out, off = [], n_lead
    for s in shapes:
        n = math.prod(s)
        out.append(flat[off:off + n].reshape(s))
        off += n
    return out


def kernel(x, meta_tokens, norm_w, w_in, sb_q_norm, sb_k_norm, gdn_conv_w, gdn_a_log, gdn_dt_bias, gdn_out_norm, hgrn_lb_logits, hgrn_out_norm, w_branch, w_out, loss_target, m_meta_tokens, m_norm_w, m_w_in, m_sb_q_norm, m_sb_k_norm, m_gdn_conv_w, m_gdn_a_log, m_gdn_dt_bias, m_gdn_out_norm, m_hgrn_lb_logits, m_hgrn_out_norm, m_w_branch, m_w_out, v_meta_tokens, v_norm_w, v_w_in, v_sb_q_norm, v_sb_k_norm, v_gdn_conv_w, v_gdn_a_log, v_gdn_dt_bias, v_gdn_out_norm, v_hgrn_lb_logits, v_hgrn_out_norm, v_w_branch, v_w_out):
    depth = norm_w.shape[0]
    small_w = dict(norm_w=norm_w, sb_q_norm=sb_q_norm, sb_k_norm=sb_k_norm, gdn_a_log=gdn_a_log,
                   gdn_dt_bias=gdn_dt_bias, gdn_out_norm=gdn_out_norm, hgrn_lb_logits=hgrn_lb_logits,
                   hgrn_out_norm=hgrn_out_norm)
    small_m = dict(zip(_SMALL, (m_norm_w, m_sb_q_norm, m_sb_k_norm, m_gdn_a_log, m_gdn_dt_bias, m_gdn_out_norm,
                                m_hgrn_lb_logits, m_hgrn_out_norm)))
    small_v = dict(zip(_SMALL, (v_norm_w, v_sb_q_norm, v_sb_k_norm, v_gdn_a_log, v_gdn_dt_bias, v_gdn_out_norm,
                                v_hgrn_lb_logits, v_hgrn_out_norm)))

    w_in_g, w_br_g, w_out_g, conv_g, meta_g = gather_over_chips(
        [w_in.astype(BF16), w_branch.astype(BF16), w_out.astype(BF16), gdn_conv_w, meta_tokens])
    w_in_full = jnp.transpose(w_in_g, (1, 2, 0, 3)).reshape(depth, D_MODEL, N_IN)
    w_br_full = jnp.transpose(w_br_g, (1, 2, 3, 0, 4)).reshape(depth, N_BRANCHES, BRANCH, D_MODEL)
    w_out_full = jnp.transpose(w_out_g, (1, 0, 2, 3)).reshape(depth, D_MODEL, D_MODEL)
    conv_full = jnp.transpose(conv_g, (1, 2, 0, 3)).reshape(depth, 4, 3 * BRANCH)
    meta_full = jnp.transpose(meta_g, (1, 0, 2)).reshape(N_META, D_MODEL)

    sq_err, grad_x, d_meta, g = device_step(
        x[0], loss_target[0], meta_full, norm_w, w_in_full, sb_q_norm, sb_k_norm, conv_full, gdn_a_log, gdn_dt_bias,
        gdn_out_norm, hgrn_lb_logits, hgrn_out_norm, w_br_full, w_out_full)

    q4 = D_MODEL // 4
    n4 = N_IN // 4
    parts = [
        jnp.transpose(g["w_in"].reshape(depth, D_MODEL, 4, n4), (2, 0, 1, 3)).reshape(4, depth * D_MODEL, n4),
        jnp.transpose(g["w_branch"].reshape(depth, N_BRANCHES, BRANCH, 4, q4), (3, 0, 1, 2, 4)).reshape(4, -1, q4),
        jnp.transpose(g["w_out"].reshape(depth, 4, q4, D_MODEL), (1, 0, 2, 3)).reshape(4, depth * q4, D_MODEL),
        jnp.transpose(g["conv_w"].reshape(depth, 4, 4, 3 * BRANCH // 4), (2, 0, 1, 3)).reshape(4, depth * 4, -1),
        jnp.transpose(d_meta.reshape(N_META, 4, q4), (1, 0, 2)),
    ]
    received = scatter_over_chips([p.astype(BF16) for p in parts])
    core_sums = [sum_slabs(r) for r in received]
    sibling_sums = swap_with_sibling(core_sums)
    sharded = [(w_in, m_w_in, v_w_in), (w_branch, m_w_branch, v_w_branch), (w_out, m_w_out, v_w_out),
               (gdn_conv_w, m_gdn_conv_w, v_gdn_conv_w), (meta_tokens, m_meta_tokens, v_meta_tokens)]
    big = []
    for a, (w_, m_, v_) in enumerate(sharded):
        mine, other = core_sums[a], sibling_sums[a]
        shp = mine.shape
        outs = adamw(f"adamw_{a}", [mine, other], w_.reshape(shp), m_.reshape(shp), v_.reshape(shp))
        big.append([o.reshape(w_.shape) for o in outs])

    small_shapes = [small_w[n].shape for n in _SMALL]
    gs = dict(norm_w=g["norm_w"], sb_q_norm=g["sb_qn"], sb_k_norm=g["sb_kn"], gdn_a_log=g["a_log"],
              gdn_dt_bias=g["dt_bias"], gdn_out_norm=g["gd_on"], hgrn_lb_logits=g["lb_logits"],
              hgrn_out_norm=g["hg_on"])
    n_lead = 128
    lead = [jnp.pad(sq_err.reshape(1), (0, n_lead - 1))]
    packed_g = gather_over_devices(_pack([gs[n] for n in _SMALL], lead))
    zeros_lead = [jnp.zeros((n_lead,), F32)]
    pw, pm, pv = (_pack([d[n] for n in _SMALL], zeros_lead) for d in (small_w, small_m, small_v))
    sg, sd, sm, sv_ = adamw("adamw_small", [packed_g[j] for j in range(8)], pw, pm, pv)
    loss = (0.5 / D_MODEL) * sg[0, 0]
    small = [dict(zip(_SMALL, _unpack(t, small_shapes, n_lead))) for t in (sg, sd, sm, sv_)]

    order = ("meta_tokens", "norm_w", "w_in", "sb_q_norm", "sb_k_norm", "gdn_conv_w", "gdn_a_log", "gdn_dt_bias",
             "gdn_out_norm", "hgrn_lb_logits", "hgrn_out_norm", "w_branch", "w_out")
    big_idx = dict(w_in=0, w_branch=1, w_out=2, gdn_conv_w=3, meta_tokens=4)
    result = [loss, grad_x[None]]
    for kind in range(4):
        for n in order:
            result.append(big[big_idx[n]][kind] if n in big_idx else small[kind][n])
    return tuple(result)
```
